```python
import math, functools
import jax, jax.numpy as jnp
from jax import lax
import numpy as np

D_MODEL = 4096
BATCH = 1
SEQ = 8192
DEPTH = 4

HA = 16
DKA = 128
DVA = 128
CONV_K = 4
CHUNK = 64
HB = 32
HB_KV = 8
DHB = 64
WINDOW = 128
BLK = 128
N_BUCKETS = 32
MAX_DIST = WINDOW
N_EXPERTS = 32
TOP_K = 4
D_EXPERT = 256
SWIGLU_LIMIT = 7.0
SWIGLU_ALPHA = 1.702
DN_ALPHA = (2 * DEPTH) ** 0.25
DN_BETA = (8 * DEPTH) ** -0.25
LN_EPS = 1e-5
NORM_EPS = 1e-6
N_MOD = 6
WA = HA * DKA
WVA = HA * DVA
WQB = HB * DHB
WKVB = HB_KV * DHB
_SPLITS = (WA, WA, WVA, WVA, HA, HA, WQB, WKVB, WKVB, D_MODEL, D_MODEL)
C_IN = WA * 2 + WVA * 2 + HA * 2 + WQB + WKVB * 2 + D_MODEL * 2

kernel_name = "hybrid_deltanet_swa_moe_deepnorm"


def _layernorm(x, g, b):
    xf = x.astype(jnp.float32)
    mu = xf.mean(-1, keepdims=True)
    var = jnp.square(xf - mu).mean(-1, keepdims=True)
    return ((xf - mu) * lax.rsqrt(var + LN_EPS) * g + b).astype(x.dtype)


def _l2norm(x):
    return x * lax.rsqrt(jnp.sum(x * x, -1, keepdims=True) + NORM_EPS)


def _causal_conv_silu(x, w):
    y = lax.conv_general_dilated(x, w[:, None, :].astype(x.dtype), window_strides=(1,),
                                 padding=((CONV_K - 1, 0),),
                                 dimension_numbers=('NHC', 'HIO', 'NHC'),
                                 feature_group_count=x.shape[-1])
    return jax.nn.silu(y)


def _gated_delta_rule(q, k, v, beta, g):
    B, S, H, Dk = q.shape
    Dv = v.shape[-1]
    N = S // CHUNK

    def chunks(t):
        return jnp.moveaxis(t.reshape((B, N, CHUNK, H) + t.shape[3:]), 3, 1)

    q, k, v, beta, g = map(chunks, (q, k, v, beta, g))
    G = jnp.cumsum(g, axis=-1)
    causal = jnp.tril(jnp.ones((CHUNK, CHUNK), bool))
    strict = jnp.tril(jnp.ones((CHUNK, CHUNK), bool), -1)
    decay = jnp.exp(jnp.where(causal, G[..., :, None] - G[..., None, :], -jnp.inf))
    kb = k * beta[..., None]
    L = jnp.where(strict, jnp.einsum('bhncd,bhnsd->bhncs', kb, k) * decay, 0.0)
    A = L + jnp.eye(CHUNK, dtype=L.dtype)
    solve = functools.partial(lax.linalg.triangular_solve, left_side=True, lower=True,
                              unit_diagonal=True)
    u = solve(A, v * beta[..., None])
    w = solve(A, kb * jnp.exp(G)[..., None])
    intra = jnp.einsum('bhncd,bhnsd->bhncs', q, k) * decay
    q_dec = q * jnp.exp(G)[..., None]
    G_last = G[..., -1]
    k_dec = k * jnp.exp(G_last[..., None] - G)[..., None]

    def step(state, xs):
        u_n, w_n, intra_n, q_n, k_n, gl_n = xs
        v_new = u_n - jnp.einsum('bhcd,bhde->bhce', w_n, state)
        o_n = (jnp.einsum('bhcd,bhde->bhce', q_n, state)
               + jnp.einsum('bhcs,bhse->bhce', intra_n, v_new))
        state = state * jnp.exp(gl_n)[..., None, None] + jnp.einsum('bhcd,bhce->bhde', k_n, v_new)
        return state, o_n

    xs = tuple(jnp.moveaxis(t, 2, 0) for t in (u, w, intra, q_dec, k_dec, G_last))
    s0 = jnp.zeros((B, H, Dk, Dv), jnp.float32)
    _, o = lax.scan(step, s0, xs)
    return o.transpose(1, 0, 3, 2, 4).reshape(B, S, H, Dv)


def _t5_bucket(d):
    n_exact = N_BUCKETS // 2
    df = jnp.maximum(d, 1).astype(jnp.float32)
    large = n_exact + (jnp.log(df / n_exact) / math.log(MAX_DIST / n_exact)
                       * (N_BUCKETS - n_exact)).astype(jnp.int32)
    large = jnp.minimum(large, N_BUCKETS - 1)
    return jnp.where(d < n_exact, d, large)


def _band_bias_and_mask(rel_bias, n_blocks):
    i = jnp.arange(BLK)[:, None]
    j = jnp.arange(2 * BLK)[None, :]
    d = i + BLK - j
    key_pos = jnp.arange(n_blocks)[:, None, None] * BLK - BLK + j
    valid = (d >= 0) & (d < WINDOW) & (key_pos >= 0)
    bucket = _t5_bucket(jnp.clip(d, 0, MAX_DIST - 1))
    bias = rel_bias.astype(jnp.float32)[bucket]
    bias = bias.transpose(2, 0, 1).reshape(HB_KV, HB // HB_KV, BLK, 2 * BLK)
    return bias, valid


def _swa_sinks(q, k, v, sinks, bias, valid):
    B, S = q.shape[:2]
    NB = S // BLK
    G = HB // HB_KV
    qb = q.reshape(B, NB, BLK, HB_KV, G, DHB)

    def band(t):
        tp = jnp.pad(t, ((0, 0), (BLK, 0), (0, 0), (0, 0))).reshape(B, NB + 1, BLK, HB_KV, DHB)
        return jnp.concatenate([tp[:, :-1], tp[:, 1:]], axis=2)

    kk, vv = band(k), band(v)
    s = jnp.einsum('bnqkgd,bnskd->bnkgqs', qb, kk).astype(jnp.float32) * (DHB ** -0.5) + bias
    s = jnp.where(valid[None, :, None, None], s, -jnp.inf)
    sink = jnp.broadcast_to(sinks.astype(jnp.float32).reshape(HB_KV, G, 1, 1), s.shape[:-1] + (1,))
    p = jax.nn.softmax(jnp.concatenate([s, sink], axis=-1), axis=-1)[..., :-1].astype(v.dtype)
    o = jnp.einsum('bnkgqs,bnskd->bnqkgd', p, vv)
    return o.reshape(B, S, HB * DHB)


def _mixer(u, w_in, conv_w, a_log, dt_bias, norm_a, sinks, w_up_a, w_up_b, w_o, bias, valid):
    B, S, _ = u.shape
    f32 = jnp.float32
    proj = u @ w_in
    qA, kA, vA, zA, bA, aA, qB, kB, vB, gA, gB = jnp.split(
        proj, [int(t) for t in np.cumsum(_SPLITS)[:-1]], axis=-1)
    qkv = _causal_conv_silu(jnp.concatenate([qA, kA, vA], axis=-1), conv_w)
    qA, kA, vA = jnp.split(qkv, [WA, 2 * WA], axis=-1)
    qA = _l2norm(qA.reshape(B, S, HA, DKA).astype(f32)) * (DKA ** -0.5)
    kA = _l2norm(kA.reshape(B, S, HA, DKA).astype(f32))
    vA = vA.reshape(B, S, HA, DVA).astype(f32)
    beta = jax.nn.sigmoid(bA.astype(f32))
    g = -jnp.exp(a_log.astype(f32)) * jax.nn.softplus(aA.astype(f32) + dt_bias.astype(f32))
    oA = _gated_delta_rule(qA, kA, vA, beta, g)
    oA = (oA * lax.rsqrt(jnp.mean(oA * oA, -1, keepdims=True) + NORM_EPS) * norm_a.astype(f32)
          * jax.nn.silu(zA.reshape(B, S, HA, DVA).astype(f32)))
    oA = oA.reshape(B, S, WVA).astype(u.dtype)
    oB = _swa_sinks(qB.reshape(B, S, HB, DHB), kB.reshape(B, S, HB_KV, DHB),
                    vB.reshape(B, S, HB_KV, DHB), sinks, bias, valid)
    merged = jax.nn.sigmoid(gA) * (oA @ w_up_a) + jax.nn.sigmoid(gB) * (oB @ w_up_b)
    return merged @ w_o


def _moe(u, w_router, b_router, w_gate_up, b_gate_up, w_down, b_down):
    f32 = jnp.float32
    logits = (u @ w_router).astype(f32) + b_router.astype(f32)
    top_val, top_idx = lax.top_k(logits, TOP_K)
    top_w = jax.nn.softmax(top_val, axis=-1)
    comb = jnp.sum(jax.nn.one_hot(top_idx, N_EXPERTS, dtype=f32) * top_w[..., None],
                   axis=-2).astype(u.dtype)
    gu = jnp.einsum('bsd,edf->bsef', u, w_gate_up) + b_gate_up
    gate, up = jnp.split(gu, 2, axis=-1)
    gate = jnp.minimum(gate, SWIGLU_LIMIT)
    up = jnp.clip(up, -SWIGLU_LIMIT, SWIGLU_LIMIT)
    h = (up + 1.0) * gate * jax.nn.sigmoid(SWIGLU_ALPHA * gate)
    return jnp.einsum('bsef,efd->bsd', h * comb[..., None], w_down) + comb @ b_down


def setup_inputs(seed: int = 0) -> dict:
    key = jax.random.key(seed)
    ks = jax.random.split(key, 26)
    f32 = jnp.float32

    def nrm(k, shape, scale):
        return jax.random.normal(k, shape, f32) * scale

    L = DEPTH
    dt = jnp.exp(jax.random.uniform(ks[8], (L, HA), f32, math.log(1e-3), math.log(1e-1)))
    return {
        "x": nrm(ks[0], (BATCH, SEQ, D_MODEL), 1.0),
        "c": nrm(ks[1], (BATCH, D_MODEL), 1.0),
        "w_ada": nrm(ks[2], (D_MODEL, N_MOD * D_MODEL), 0.5 * D_MODEL ** -0.5),
        "b_ada": nrm(ks[3], (N_MOD * D_MODEL,), 0.01),
        "ada_table": nrm(ks[4], (L, N_MOD, D_MODEL), 0.02),
        "rel_bias": nrm(ks[5], (N_BUCKETS, HB), 0.5),
        "w_in": nrm(ks[6], (L, D_MODEL, C_IN), D_MODEL ** -0.5),
        "conv_w": nrm(ks[7], (L, CONV_K, 3 * WA), CONV_K ** -0.5),
        "a_log": jnp.log(jax.random.uniform(ks[9], (L, HA), f32, 1.0, 16.0)),
        "dt_bias": jnp.log(jnp.expm1(dt)),
        "norm_a": 1.0 + nrm(ks[10], (L, DVA), 0.01),
        "sinks": nrm(ks[11], (L, HB), 1.0),
        "w_up_a": nrm(ks[12], (L, WVA, D_MODEL), WVA ** -0.5),
        "w_up_b": nrm(ks[13], (L, WQB, D_MODEL), WQB ** -0.5),
        "w_o": nrm(ks[14], (L, D_MODEL, D_MODEL), DN_BETA * D_MODEL ** -0.5),
        "ln1_g": 1.0 + nrm(ks[15], (L, D_MODEL), 0.01),
        "ln1_b": nrm(ks[16], (L, D_MODEL), 0.01),
        "w_router": nrm(ks[17], (L, D_MODEL, N_EXPERTS), D_MODEL ** -0.5),
        "b_router": nrm(ks[18], (L, N_EXPERTS), 0.01),
        "w_gate_up": nrm(ks[19], (L, N_EXPERTS, D_MODEL, 2 * D_EXPERT), D_MODEL ** -0.5),
        "b_gate_up": nrm(ks[20], (L, N_EXPERTS, 2 * D_EXPERT), 0.01),
        "w_down": nrm(ks[21], (L, N_EXPERTS, D_EXPERT, D_MODEL), DN_BETA * D_EXPERT ** -0.5),
        "b_down": nrm(ks[22], (L, N_EXPERTS, D_MODEL), 0.01),
        "ln2_g": 1.0 + nrm(ks[23], (L, D_MODEL), 0.01),
        "ln2_b": nrm(ks[24], (L, D_MODEL), 0.01),
    }


def reference(x, c, w_ada, b_ada, ada_table, rel_bias, w_in, conv_w, a_log, dt_bias, norm_a,
              sinks, w_up_a, w_up_b, w_o, ln1_g, ln1_b, w_router, b_router, w_gate_up,
              b_gate_up, w_down, b_down, ln2_g, ln2_b):
    B, S, D = x.shape
    mod_base = (jax.nn.silu(c) @ w_ada + b_ada).reshape(B, N_MOD, D)
    bias, valid = _band_bias_and_mask(rel_bias, S // BLK)
    for l in range(DEPTH):
        mod = (mod_base + ada_table[l])[:, :, None, :]
        sh1, sc1, gt1, sh2, sc2, gt2 = jnp.moveaxis(mod, 1, 0)
        u = x * (1.0 + sc1) + sh1
        y = _mixer(u, w_in[l], conv_w[l], a_log[l], dt_bias[l], norm_a[l], sinks[l],
                   w_up_a[l], w_up_b[l], w_o[l], bias, valid)
        x = _layernorm(DN_ALPHA * x + gt1 * y, ln1_g[l], ln1_b[l])
        u = x * (1.0 + sc2) + sh2
        y = _moe(u, w_router[l], b_router[l], w_gate_up[l], b_gate_up[l], w_down[l], b_down[l])
        x = _layernorm(DN_ALPHA * x + gt2 * y, ln2_g[l], ln2_b[l])
    return x
```

```python
import functools
import math

import jax
import jax.numpy as jnp
import numpy as np
from jax import lax
from jax.experimental import pallas as pl
from jax.experimental.pallas import tpu as pltpu

D_MODEL = 4096
DEPTH = 4
HA = 16
DKA = 128
DVA = 128
CONV_K = 4
HB = 32
HB_KV = 8
DHB = 64
WINDOW = 128
BLK = 128
N_BUCKETS = 32
MAX_DIST = WINDOW
N_EXPERTS = 32
TOP_K = 4
D_EXPERT = 256
SWIGLU_LIMIT = 7.0
SWIGLU_ALPHA = 1.702
DN_ALPHA = (2 * DEPTH) ** 0.25
LN_EPS = 1e-5
NORM_EPS = 1e-6
N_MOD = 6
WA = HA * DKA
WVA = HA * DVA
WQB = HB * DHB
WKVB = HB_KV * DHB
OFF_BA = 2 * WA + 2 * WVA
OFF_B = OFF_BA + 2 * HA
W_PROJ_A = OFF_BA
W_PROJ_B = WQB + 2 * WKVB + 2 * D_MODEL

LANES = 128
CHUNK = 128
VMEM_LIMIT = 56 * 1024 * 1024
NEG_BIG = -1e30

bf16 = jnp.bfloat16
f32 = jnp.float32


def _dot(a, b):
    return jnp.dot(a, b, preferred_element_type=f32)


def _dot_nt(a, b):
    return lax.dot_general(a, b, (((1,), (1,)), ((), ())), preferred_element_type=f32)


def _split2(x):
    hi = x.astype(bf16)
    lo = (x - hi.astype(f32)).astype(bf16)
    return hi, lo


def _dot3(a, b):
    ah, al = _split2(a)
    bh, bl = _split2(b)
    return _dot(ah, bh) + _dot(ah, bl) + _dot(al, bh)


def _params(sem):
    return pltpu.CompilerParams(dimension_semantics=sem, vmem_limit_bytes=VMEM_LIMIT)


def _ada_kernel(c_ref, w_ref, b_ref, o_ref):
    c = c_ref[...]
    a = c * jax.nn.sigmoid(c)
    o_ref[...] = _dot3(a, w_ref[...]) + b_ref[...]


def _ada(c8, w_ada, b_ada):
    n = w_ada.shape[1]
    tn = 512
    return pl.pallas_call(
        _ada_kernel,
        grid=(n // tn,),
        in_specs=[pl.BlockSpec((8, D_MODEL), lambda j: (0, 0)),
                  pl.BlockSpec((D_MODEL, tn), lambda j: (0, j)),
                  pl.BlockSpec((1, tn), lambda j: (0, j))],
        out_specs=pl.BlockSpec((8, tn), lambda j: (0, j)),
        out_shape=jax.ShapeDtypeStruct((8, n), f32),
        compiler_params=_params(("parallel",)),
        name="ada_mod",
    )(c8, w_ada, b_ada.reshape(1, n))


def _modulate_kernel(x_ref, sc_ref, sh_ref, u_ref):
    u_ref[...] = (x_ref[...] * (1.0 + sc_ref[...]) + sh_ref[...]).astype(u_ref.dtype)


def _modulate(x, sc, sh):
    s, d = x.shape
    tm = min(256, s)
    row = pl.BlockSpec((1, d), lambda i: (0, 0))
    return pl.pallas_call(
        _modulate_kernel,
        grid=(s // tm,),
        in_specs=[pl.BlockSpec((tm, d), lambda i: (i, 0)), row, row],
        out_specs=pl.BlockSpec((tm, d), lambda i: (i, 0)),
        out_shape=jax.ShapeDtypeStruct((s, d), bf16),
        compiler_params=_params(("parallel",)),
        name="modulate",
    )(x, sc, sh)


def _mm_kernel(a_ref, b_ref, o_ref):
    o_ref[...] = _dot(a_ref[...], b_ref[...]).astype(o_ref.dtype)


def _matmul(a, w, layer, col0, ncols, out_dtype, tm, tn, name):
    m, k = a.shape
    tm = min(tm, m)
    tn = min(tn, ncols)
    jb = col0 // tn
    return pl.pallas_call(
        _mm_kernel,
        grid=(m // tm, ncols // tn),
        in_specs=[pl.BlockSpec((tm, k), lambda i, j: (i, 0)),
                  pl.BlockSpec((None, k, tn), lambda i, j: (layer, 0, j + jb))],
        out_specs=pl.BlockSpec((tm, tn), lambda i, j: (i, j)),
        out_shape=jax.ShapeDtypeStruct((m, ncols), out_dtype),
        compiler_params=_params(("parallel", "parallel")),
        name=name,
    )(a, w)


def _merge_kernel(oa_ref, ob_ref, wa_ref, wb_ref, ga_ref, gb_ref, o_ref):
    ya = _dot(oa_ref[...], wa_ref[...])
    yb = _dot(ob_ref[...], wb_ref[...])
    o_ref[...] = (jax.nn.sigmoid(ga_ref[...]) * ya
                  + jax.nn.sigmoid(gb_ref[...]) * yb).astype(o_ref.dtype)


def _merge(oa, ob, w_up_a, w_up_b, proj_b, layer):
    s = oa.shape[0]
    tm = min(1024, s)
    tn = 512
    ga0 = (WQB + 2 * WKVB) // tn
    gb0 = ga0 + D_MODEL // tn
    return pl.pallas_call(
        _merge_kernel,
        grid=(s // tm, D_MODEL // tn),
        in_specs=[pl.BlockSpec((tm, WVA), lambda i, j: (i, 0)),
                  pl.BlockSpec((tm, WQB), lambda i, j: (i, 0)),
                  pl.BlockSpec((None, WVA, tn), lambda i, j: (layer, 0, j)),
                  pl.BlockSpec((None, WQB, tn), lambda i, j: (layer, 0, j)),
                  pl.BlockSpec((tm, tn), lambda i, j: (i, j + ga0)),
                  pl.BlockSpec((tm, tn), lambda i, j: (i, j + gb0))],
        out_specs=pl.BlockSpec((tm, tn), lambda i, j: (i, j)),
        out_shape=jax.ShapeDtypeStruct((s, D_MODEL), bf16),
        compiler_params=_params(("parallel", "parallel")),
        name="gated_merge",
    )(oa, ob, w_up_a, w_up_b, proj_b, proj_b)


def _ln_core(x_ref, y_ref, gt_ref, g_ref, b_ref):
    z = DN_ALPHA * x_ref[...] + gt_ref[...] * y_ref[...]
    mu = jnp.mean(z, axis=-1, keepdims=True)
    zc = z - mu
    var = jnp.mean(zc * zc, axis=-1, keepdims=True)
    return zc * lax.rsqrt(var + LN_EPS) * g_ref[...] + b_ref[...]


def _ln_kernel(x_ref, y_ref, gt_ref, g_ref, b_ref, sc_ref, sh_ref, xo_ref, uo_ref):
    xn = _ln_core(x_ref, y_ref, gt_ref, g_ref, b_ref)
    xo_ref[...] = xn
    uo_ref[...] = (xn * (1.0 + sc_ref[...]) + sh_ref[...]).astype(uo_ref.dtype)


def _route(logits):
    n_e = logits.shape[-1]
    lane = lax.broadcasted_iota(jnp.int32, logits.shape, 1)
    work = logits
    picked = jnp.zeros(logits.shape, f32)
    top = None
    for r in range(TOP_K):
        m = jnp.max(work, axis=-1, keepdims=True)
        if r == 0:
            top = m
        idx = jnp.min(jnp.where(work == m, lane, n_e), axis=-1, keepdims=True)
        hit = lane == idx
        picked = jnp.where(hit, 1.0, picked)
        work = jnp.where(hit, -jnp.inf, work)
    e = jnp.where(picked > 0.0, jnp.exp(logits - top), 0.0)
    return e / jnp.sum(e, axis=-1, keepdims=True)


def _ln_route_kernel(x_ref, y_ref, gt_ref, g_ref, b_ref, sc_ref, sh_ref, wr_ref, br_ref,
                     xo_ref, uo_ref, comb_ref):
    xn = _ln_core(x_ref, y_ref, gt_ref, g_ref, b_ref)
    xo_ref[...] = xn
    u = xn * (1.0 + sc_ref[...]) + sh_ref[...]
    uo_ref[...] = u.astype(uo_ref.dtype)
    logits = _dot3(u, wr_ref[...]) + br_ref[...]
    comb_ref[...] = _route(logits)


def _ln(x, y, gt, g, b, sc, sh, layer, router=None):
    s, d = x.shape
    tm = min(256, s)
    tile = pl.BlockSpec((tm, d), lambda i: (i, 0))
    row = pl.BlockSpec((1, d), lambda i: (0, 0))
    lrow = pl.BlockSpec((None, 1, d), lambda i: (layer, 0, 0))
    in_specs = [tile, tile, row, lrow, lrow, row, row]
    args = [x, y, gt, g, b, sc, sh]
    out_specs = [tile, tile]
    out_shape = [jax.ShapeDtypeStruct((s, d), f32), jax.ShapeDtypeStruct((s, d), bf16)]
    if router is None:
        body, name = _ln_kernel, "residual_ln"
    else:
        body, name = _ln_route_kernel, "residual_ln_route"
        w_router, b_router = router
        in_specs += [pl.BlockSpec((None, d, N_EXPERTS), lambda i: (layer, 0, 0)),
                     pl.BlockSpec((None, 1, N_EXPERTS), lambda i: (layer, 0, 0))]
        args += [w_router, b_router]
        out_specs.append(pl.BlockSpec((tm, N_EXPERTS), lambda i: (i, 0)))
        out_shape.append(jax.ShapeDtypeStruct((s, N_EXPERTS), f32))
    return pl.pallas_call(
        body,
        grid=(s // tm,),
        in_specs=in_specs,
        out_specs=out_specs,
        out_shape=out_shape,
        compiler_params=_params(("parallel",)),
        name=name,
    )(*args)


def _moe_kernel(u_ref, comb_ref, wgu_ref, bgu_ref, wd_ref, bd_ref, o_ref):
    e = pl.program_id(1)
    comb = comb_ref[...]

    @pl.when(e == 0)
    def _():
        o_ref[...] = _dot3(comb, bd_ref[...])

    gu = _dot(u_ref[...], wgu_ref[...]) + bgu_ref[...]
    gate = jnp.minimum(gu[:, :D_EXPERT], SWIGLU_LIMIT)
    up = jnp.clip(gu[:, D_EXPERT:], -SWIGLU_LIMIT, SWIGLU_LIMIT)
    h = (up + 1.0) * gate * jax.nn.sigmoid(SWIGLU_ALPHA * gate)
    lane = lax.broadcasted_iota(jnp.int32, comb.shape, 1)
    c = jnp.sum(jnp.where(lane == e, comb, 0.0), axis=1, keepdims=True)
    o_ref[...] += _dot((h * c).astype(bf16), wd_ref[...])


def _moe(u, comb, w_gate_up, b_gate_up, w_down, b_down, layer):
    s, d = u.shape
    tm = min(512, s)
    return pl.pallas_call(
        _moe_kernel,
        grid=(s // tm, N_EXPERTS),
        in_specs=[pl.BlockSpec((tm, d), lambda i, e: (i, 0)),
                  pl.BlockSpec((tm, N_EXPERTS), lambda i, e: (i, 0)),
                  pl.BlockSpec((None, None, d, 2 * D_EXPERT), lambda i, e: (layer, e, 0, 0)),
                  pl.BlockSpec((None, None, 1, 2 * D_EXPERT), lambda i, e: (layer, e, 0, 0)),
                  pl.BlockSpec((None, None, D_EXPERT, d), lambda i, e: (layer, e, 0, 0)),
                  pl.BlockSpec((None, N_EXPERTS, d), lambda i, e: (layer, 0, 0))],
        out_specs=pl.BlockSpec((tm, d), lambda i, e: (i, 0)),
        out_shape=jax.ShapeDtypeStruct((s, d), f32),
        compiler_params=_params(("parallel", "arbitrary")),
        name="moe_experts",
    )(u, comb, w_gate_up, b_gate_up, w_down, b_down)


def _silu(x):
    return x * jax.nn.sigmoid(x)


def _l2norm(x):
    return x * lax.rsqrt(jnp.sum(x * x, axis=-1, keepdims=True) + NORM_EPS)


def _delta_kernel(heads, rows,
                  q_ref, k_ref, v_ref, z_ref, qh_ref, kh_ref, vh_ref, b_ref, a_ref,
                  cq_ref, ck_ref, cv_ref, alog_ref, dt_ref, na_ref,
                  o_ref, state_ref, xs_ref):
    hp = pl.program_id(0)
    i = pl.program_id(1)
    n_chunks = rows // CHUNK

    @pl.when(i == 0)
    def _():
        state_ref[...] = jnp.zeros_like(state_ref)

    first = i == 0
    row_i = lax.broadcasted_iota(jnp.int32, (CHUNK, CHUNK), 0)
    col_i = lax.broadcasted_iota(jnp.int32, (CHUNK, CHUNK), 1)
    causal = row_i >= col_i
    strict = row_i > col_i
    tril = jnp.where(causal, 1.0, 0.0).astype(bf16)
    eye = jnp.where(row_i == col_i, 1.0, 0.0)
    merge_masks = []
    for shift in range(int(math.log2(CHUNK))):
        rb = row_i >> shift
        merge_masks.append(jnp.logical_and((rb & 1) == 1, (col_i >> shift) == rb - 1))
    lane = lax.broadcasted_iota(jnp.int32, (rows, LANES), 1)

    beta_all = jax.nn.sigmoid(b_ref[...])
    xa = a_ref[...] + dt_ref[...]
    softplus = jnp.maximum(xa, 0.0) + jnp.log(1.0 + jnp.exp(-jnp.abs(xa)))
    g_all = -jnp.exp(alog_ref[...]) * softplus

    def conv_silu(x_ref, halo_ref, cw_ref, sl):
        xs_ref[0:8, :] = jnp.where(first, 0.0, halo_ref[:, sl])
        xs_ref[8:8 + rows, :] = x_ref[:, sl]
        acc = cw_ref[CONV_K - 1:CONV_K, sl] * xs_ref[8:8 + rows, :]
        for back in range(1, CONV_K):
            tap = CONV_K - 1 - back
            acc = acc + cw_ref[tap:tap + 1, sl] * xs_ref[8 - back:8 - back + rows, :]
        return _silu(acc)

    for j in range(heads):
        sl = slice(j * LANES, (j + 1) * LANES)
        head = hp * heads + j
        q = _l2norm(conv_silu(q_ref, qh_ref, cq_ref, sl)) * (DKA ** -0.5)
        k = _l2norm(conv_silu(k_ref, kh_ref, ck_ref, sl))
        v = conv_silu(v_ref, vh_ref, cv_ref, sl)
        pick = lane == head
        beta_c = jnp.sum(jnp.where(pick, beta_all, 0.0), axis=1, keepdims=True)
        g_c = jnp.sum(jnp.where(pick, g_all, 0.0), axis=1, keepdims=True)
        state = state_ref[j]
        for c in range(n_chunks):
            r = slice(c * CHUNK, (c + 1) * CHUNK)
            qc, kc, vc = q[r], k[r], v[r]
            beta_b = jnp.broadcast_to(beta_c[r], (CHUNK, CHUNK))
            g_b = jnp.broadcast_to(g_c[r], (CHUNK, CHUNK))
            g1 = g_b.astype(bf16)
            r1 = g_b - g1.astype(f32)
            g2 = r1.astype(bf16)
            g3 = (r1 - g2.astype(f32)).astype(bf16)
            cum = _dot(tril, g1) + _dot(tril, g2) + _dot(tril, g3)
            cum_t = cum.T
            decay = jnp.exp(jnp.where(causal, cum - cum_t, NEG_BIG))
            e_g = jnp.exp(cum)
            g_last = cum[CHUNK - 1:CHUNK, :]
            e_rest = jnp.exp(g_last - cum)
            kb = kc * beta_b
            kb16 = kb.astype(bf16)
            k16 = kc.astype(bf16)
            lmat = jnp.where(strict, _dot_nt(kb16, k16) * decay, 0.0)
            tn = jnp.where(merge_masks[0], -lmat, 0.0)
            for mask in merge_masks[1:]:
                t_inv = (eye + tn).astype(bf16)
                sub = jnp.where(mask, lmat, 0.0).astype(bf16)
                tn = tn - _dot(_dot(t_inv, sub).astype(bf16), t_inv)
            rhs = jnp.concatenate([vc * beta_b, kb * e_g], axis=1)
            uw = rhs + _dot(tn.astype(bf16), rhs.astype(bf16))
            u_c = uw[:, :DVA]
            w_c = uw[:, DVA:]
            intra = jnp.where(causal, _dot_nt(qc.astype(bf16), k16) * decay, 0.0)
            q_dec = qc * e_g
            k_dec_t = (kc * e_rest).T.astype(bf16)
            s16 = state.astype(bf16)
            ws = _dot(jnp.concatenate([w_c, q_dec], axis=0).astype(bf16), s16)
            v_new = u_c - ws[:CHUNK]
            v16 = v_new.astype(bf16)
            o_c = ws[CHUNK:] + _dot(intra.astype(bf16), v16)
            state = state * jnp.exp(g_last) + _dot(k_dec_t, v16)
            zc = z_ref[r, sl]
            o_n = o_c * lax.rsqrt(jnp.mean(o_c * o_c, axis=-1, keepdims=True) + NORM_EPS)
            o_ref[r, sl] = (o_n * na_ref[...] * _silu(zc)).astype(o_ref.dtype)
        state_ref[j] = state


def _delta(proj_a, ba, conv_w, alog, dtb, norm_a, layer):
    s = proj_a.shape[0]
    heads = 2
    rows = min(512, s)
    hw = heads * LANES
    nq = WA // hw
    halo_blocks = rows // 8

    def tile(col0):
        return pl.BlockSpec((rows, hw), lambda hp, i: (i, hp + col0 // hw))

    def halo(col0):
        return pl.BlockSpec(
            (8, hw), lambda hp, i: (jnp.maximum(i * halo_blocks - 1, 0), hp + col0 // hw))

    def cw(col0):
        return pl.BlockSpec((None, CONV_K, hw), lambda hp, i: (layer, 0, hp + col0 // hw))

    lrow = pl.BlockSpec((None, 1, LANES), lambda hp, i: (layer, 0, 0))
    return pl.pallas_call(
        functools.partial(_delta_kernel, heads, rows),
        grid=(nq, s // rows),
        in_specs=[tile(0), tile(WA), tile(2 * WA), tile(2 * WA + WVA),
                  halo(0), halo(WA), halo(2 * WA),
                  pl.BlockSpec((rows, LANES), lambda hp, i: (i, 0)),
                  pl.BlockSpec((rows, LANES), lambda hp, i: (i, 1)),
                  cw(0), cw(WA), cw(2 * WA), lrow, lrow, lrow],
        out_specs=pl.BlockSpec((rows, hw), lambda hp, i: (i, hp)),
        out_shape=jax.ShapeDtypeStruct((s, WVA), bf16),
        scratch_shapes=[pltpu.VMEM((heads, DKA, DVA), f32),
                        pltpu.VMEM((rows + 8, LANES), f32)],
        compiler_params=_params(("parallel", "arbitrary")),
        name="gated_delta",
    )(proj_a, proj_a, proj_a, proj_a, proj_a, proj_a, proj_a, ba, ba,
      conv_w, conv_w, conv_w, alog, dtb, norm_a)


def _attn_kernel(layer, q_ref, kp_ref, kc_ref, vp_ref, vc_ref, bm_ref, sink_ref, o_ref):
    n = pl.program_id(0)
    lane = lax.broadcasted_iota(jnp.int32, (2 * BLK, LANES), 1)
    low = lane < DHB
    key_j = lax.broadcasted_iota(jnp.int32, (BLK, 2 * BLK), 1)
    no_prev = jnp.logical_and(n == 0, key_j < BLK)
    scale = DHB ** -0.5
    groups_per_tile = LANES // DHB
    q_per_kv = HB // HB_KV
    for t in range(WKVB // LANES):
        sl = slice(t * LANES, (t + 1) * LANES)
        k_t = jnp.concatenate([kp_ref[:, sl], kc_ref[:, sl]], axis=0)
        v_t = jnp.concatenate([vp_ref[:, sl], vc_ref[:, sl]], axis=0)
        for gg in range(groups_per_tile):
            grp = t * groups_per_tile + gg
            mine = low if gg == 0 else jnp.logical_not(low)
            k_m = jnp.where(mine, k_t, 0.0)
            v_m = jnp.where(mine, v_t, 0.0)
            k_2 = k_m + pltpu.roll(k_m, DHB, axis=1)
            v_2 = v_m + pltpu.roll(v_m, DHB, axis=1)
            k_cat = jnp.concatenate([jnp.where(low, k_2, 0.0),
                                     jnp.where(low, 0.0, k_2)], axis=0).astype(bf16)
            v_cat = jnp.concatenate([jnp.where(low, v_2, 0.0),
                                     jnp.where(low, 0.0, v_2)], axis=0).astype(bf16)
            for pp in range(q_per_kv // 2):
                pair = grp * (q_per_kv // 2) + pp
                ql = slice(pair * LANES, (pair + 1) * LANES)
                s = _dot_nt(q_ref[:, ql].astype(bf16), k_cat) * scale
                probs = []
                for hh in range(2):
                    head = 2 * pair + hh
                    sh = s[:, hh * 2 * BLK:(hh + 1) * 2 * BLK] + bm_ref[head]
                    sh = jnp.where(no_prev, NEG_BIG, sh)
                    sink = sink_ref[layer, head]
                    mx = jnp.maximum(jnp.max(sh, axis=-1, keepdims=True), sink)
                    e = jnp.exp(sh - mx)
                    den = jnp.sum(e, axis=-1, keepdims=True) + jnp.exp(sink - mx)
                    probs.append((e * (1.0 / den)).astype(bf16))
                o_ref[:, ql] = _dot(jnp.concatenate(probs, axis=1), v_cat).astype(o_ref.dtype)


def _attn(proj_b, bias_mask, sinks, layer):
    s = proj_b.shape[0]
    kb0 = WQB // WKVB
    return pl.pallas_call(
        functools.partial(_attn_kernel, layer),
        grid=(s // BLK,),
        in_specs=[pl.BlockSpec((BLK, WQB), lambda n: (n, 0)),
                  pl.BlockSpec((BLK, WKVB), lambda n: (jnp.maximum(n - 1, 0), kb0)),
                  pl.BlockSpec((BLK, WKVB), lambda n: (n, kb0)),
                  pl.BlockSpec((BLK, WKVB), lambda n: (jnp.maximum(n - 1, 0), kb0 + 1)),
                  pl.BlockSpec((BLK, WKVB), lambda n: (n, kb0 + 1)),
                  pl.BlockSpec((HB, BLK, 2 * BLK), lambda n: (0, 0, 0)),
                  pl.BlockSpec(memory_space=pltpu.SMEM)],
        out_specs=pl.BlockSpec((BLK, WQB), lambda n: (n, 0)),
        out_shape=jax.ShapeDtypeStruct((s, WQB), bf16),
        compiler_params=_params(("parallel",)),
        name="swa_attention",
    )(proj_b, proj_b, proj_b, proj_b, proj_b, bias_mask, sinks)


def _t5_bucket(d):
    n_exact = N_BUCKETS // 2
    df = jnp.maximum(d, 1).astype(f32)
    large = n_exact + (jnp.log(df / n_exact) / math.log(MAX_DIST / n_exact)
                       * (N_BUCKETS - n_exact)).astype(jnp.int32)
    large = jnp.minimum(large, N_BUCKETS - 1)
    return jnp.where(d < n_exact, d, large)


def _band_bias(rel_bias):
    i = jnp.arange(BLK)[:, None]
    j = jnp.arange(2 * BLK)[None, :]
    d = i + BLK - j
    inside = (d >= 0) & (d < WINDOW)
    bucket = _t5_bucket(jnp.clip(d, 0, MAX_DIST - 1))
    bias = rel_bias.astype(f32)[bucket].transpose(2, 0, 1)
    return jnp.where(inside[None], bias, NEG_BIG)


def kernel(x, c, w_ada, b_ada, ada_table, rel_bias, w_in, conv_w, a_log, dt_bias, norm_a, sinks,
           w_up_a, w_up_b, w_o, ln1_g, ln1_b, w_router, b_router, w_gate_up, b_gate_up, w_down,
           b_down, ln2_g, ln2_b):
    bsz, s, d = x.shape
    assert bsz == 1 and d == D_MODEL and s % BLK == 0
    depth = w_in.shape[0]
    xs = x.reshape(s, d)

    mod_base = _ada(jnp.broadcast_to(c, (8, d)), w_ada, b_ada)[0]
    mod = mod_base.reshape(1, N_MOD, d) + ada_table

    w_a = w_in[:, :, :W_PROJ_A].astype(bf16)
    w_b = w_in[:, :, OFF_B:].astype(bf16)
    w_ba = w_in[:, :, OFF_BA:OFF_B]
    w_ba = jnp.concatenate(
        [jnp.pad(w_ba[:, :, :HA], ((0, 0), (0, 0), (0, LANES - HA))),
         jnp.pad(w_ba[:, :, HA:], ((0, 0), (0, 0), (0, LANES - HA)))], axis=-1).astype(bf16)
    w_ua = w_up_a.astype(bf16)
    w_ub = w_up_b.astype(bf16)
    w_o16 = w_o.astype(bf16)
    w_gu = w_gate_up.astype(bf16)
    w_dn = w_down.astype(bf16)
    b_gu = b_gate_up.reshape(depth, N_EXPERTS, 1, 2 * D_EXPERT)
    b_rt = b_router.reshape(depth, 1, N_EXPERTS)
    pad_h = ((0, 0), (0, LANES - HA))
    alog = jnp.pad(a_log, pad_h).reshape(depth, 1, LANES)
    dtb = jnp.pad(dt_bias, pad_h).reshape(depth, 1, LANES)
    na = norm_a.reshape(depth, 1, DVA)
    g1 = ln1_g.reshape(depth, 1, d)
    b1 = ln1_b.reshape(depth, 1, d)
    g2 = ln2_g.reshape(depth, 1, d)
    b2 = ln2_b.reshape(depth, 1, d)
    bias_mask = _band_bias(rel_bias)

    def mrow(layer, idx):
        return mod[layer, idx].reshape(1, d)

    u = _modulate(xs, mrow(0, 1), mrow(0, 0))
    for layer in range(depth):
        proj_a = _matmul(u, w_a, layer, 0, W_PROJ_A, f32, 1024, 1024, "in_proj_a")
        proj_b = _matmul(u, w_b, layer, 0, W_PROJ_B, f32, 1024, 1024, "in_proj_b")
        ba = _matmul(u, w_ba, layer, 0, 2 * LANES, f32, 1024, 2 * LANES, "in_proj_ba")
        o_a = _delta(proj_a, ba, conv_w, alog, dtb, na, layer)
        o_b = _attn(proj_b, bias_mask, sinks, layer)
        merged = _merge(o_a, o_b, w_ua, w_ub, proj_b, layer)
        y = _matmul(merged, w_o16, layer, 0, d, f32, 1024, 1024, "out_proj")
        xs, u, comb = _ln(xs, y, mrow(layer, 2), g1, b1, mrow(layer, 4), mrow(layer, 3), layer,
                          router=(w_router, b_rt))
        y = _moe(u, comb, w_gu, b_gu, w_dn, b_down, layer)
        nxt = min(layer + 1, depth - 1)
        xs, u = _ln(xs, y, mrow(layer, 5), g2, b2, mrow(nxt, 1), mrow(nxt, 0), layer)
    return xs.reshape(bsz, s, d)
```

```python
import functools
import math

import jax
import jax.numpy as jnp
import numpy as np
from jax import lax
from jax.experimental import pallas as pl
from jax.experimental.pallas import tpu as pltpu

D_MODEL = 4096
DEPTH = 4
HA = 16
DKA = 128
DVA = 128
CONV_K = 4
HB = 32
HB_KV = 8
DHB = 64
WINDOW = 128
BLK = 128
N_BUCKETS = 32
MAX_DIST = WINDOW
N_EXPERTS = 32
TOP_K = 4
D_EXPERT = 256
SWIGLU_LIMIT = 7.0
SWIGLU_ALPHA = 1.702
DN_ALPHA = (2 * DEPTH) ** 0.25
LN_EPS = 1e-5
NORM_EPS = 1e-6
N_MOD = 6
WA = HA * DKA
WVA = HA * DVA
WQB = HB * DHB
WKVB = HB_KV * DHB
OFF_BA = 2 * WA + 2 * WVA
OFF_B = OFF_BA + 2 * HA
W_PROJ_A = OFF_BA
W_PROJ_B = WQB + 2 * WKVB + 2 * D_MODEL

LANES = 128
CHUNK = 128
VMEM_LIMIT = 56 * 1024 * 1024
NEG_BIG = -1e30

bf16 = jnp.bfloat16
f32 = jnp.float32


def _dot(a, b):
    return jnp.dot(a, b, preferred_element_type=f32)


def _dot_nt(a, b):
    return lax.dot_general(a, b, (((1,), (1,)), ((), ())), preferred_element_type=f32)


def _split2(x):
    hi = x.astype(bf16)
    lo = (x - hi.astype(f32)).astype(bf16)
    return hi, lo


def _dot3(a, b):
    ah, al = _split2(a)
    bh, bl = _split2(b)
    return _dot(ah, bh) + _dot(ah, bl) + _dot(al, bh)


def _params(sem):
    return pltpu.CompilerParams(dimension_semantics=sem, vmem_limit_bytes=VMEM_LIMIT)


def _ada_kernel(c_ref, w_ref, b_ref, o_ref):
    c = c_ref[...]
    a = c * jax.nn.sigmoid(c)
    o_ref[...] = _dot3(a, w_ref[...]) + b_ref[...]


def _ada(c8, w_ada, b_ada):
    n = w_ada.shape[1]
    tn = 512
    return pl.pallas_call(
        _ada_kernel,
        grid=(n // tn,),
        in_specs=[pl.BlockSpec((8, D_MODEL), lambda j: (0, 0)),
                  pl.BlockSpec((D_MODEL, tn), lambda j: (0, j)),
                  pl.BlockSpec((1, tn), lambda j: (0, j))],
        out_specs=pl.BlockSpec((8, tn), lambda j: (0, j)),
        out_shape=jax.ShapeDtypeStruct((8, n), f32),
        compiler_params=_params(("parallel",)),
        name="ada_mod",
    )(c8, w_ada, b_ada.reshape(1, n))


def _modulate_kernel(x_ref, sc_ref, sh_ref, u_ref):
    u_ref[...] = (x_ref[...] * (1.0 + sc_ref[...]) + sh_ref[...]).astype(u_ref.dtype)


def _modulate(x, sc, sh):
    s, d = x.shape
    tm = min(256, s)
    row = pl.BlockSpec((1, d), lambda i: (0, 0))
    return pl.pallas_call(
        _modulate_kernel,
        grid=(s // tm,),
        in_specs=[pl.BlockSpec((tm, d), lambda i: (i, 0)), row, row],
        out_specs=pl.BlockSpec((tm, d), lambda i: (i, 0)),
        out_shape=jax.ShapeDtypeStruct((s, d), bf16),
        compiler_params=_params(("parallel",)),
        name="modulate",
    )(x, sc, sh)


def _mm_kernel(a_ref, b_ref, o_ref):
    o_ref[...] = _dot(a_ref[...], b_ref[...]).astype(o_ref.dtype)


def _matmul(a, w, layer, col0, ncols, out_dtype, tm, tn, name):
    m, k = a.shape
    tm = min(tm, m)
    tn = min(tn, ncols)
    jb = col0 // tn
    return pl.pallas_call(
        _mm_kernel,
        grid=(m // tm, ncols // tn),
        in_specs=[pl.BlockSpec((tm, k), lambda i, j: (i, 0)),
                  pl.BlockSpec((None, k, tn), lambda i, j: (layer, 0, j + jb))],
        out_specs=pl.BlockSpec((tm, tn), lambda i, j: (i, j)),
        out_shape=jax.ShapeDtypeStruct((m, ncols), out_dtype),
        compiler_params=_params(("parallel", "parallel")),
        name=name,
    )(a, w)


def _merge_kernel(oa_ref, ob_ref, wa_ref, wb_ref, ga_ref, gb_ref, o_ref):
    ya = _dot(oa_ref[...], wa_ref[...])
    yb = _dot(ob_ref[...], wb_ref[...])
    o_ref[...] = (jax.nn.sigmoid(ga_ref[...]) * ya
                  + jax.nn.sigmoid(gb_ref[...]) * yb).astype(o_ref.dtype)


def _merge(oa, ob, w_up_a, w_up_b, proj_b, layer):
    s = oa.shape[0]
    tm = min(1024, s)
    tn = 512
    ga0 = (WQB + 2 * WKVB) // tn
    gb0 = ga0 + D_MODEL // tn
    return pl.pallas_call(
        _merge_kernel,
        grid=(s // tm, D_MODEL // tn),
        in_specs=[pl.BlockSpec((tm, WVA), lambda i, j: (i, 0)),
                  pl.BlockSpec((tm, WQB), lambda i, j: (i, 0)),
                  pl.BlockSpec((None, WVA, tn), lambda i, j: (layer, 0, j)),
                  pl.BlockSpec((None, WQB, tn), lambda i, j: (layer, 0, j)),
                  pl.BlockSpec((tm, tn), lambda i, j: (i, j + ga0)),
                  pl.BlockSpec((tm, tn), lambda i, j: (i, j + gb0))],
        out_specs=pl.BlockSpec((tm, tn), lambda i, j: (i, j)),
        out_shape=jax.ShapeDtypeStruct((s, D_MODEL), bf16),
        compiler_params=_params(("parallel", "parallel")),
        name="gated_merge",
    )(oa, ob, w_up_a, w_up_b, proj_b, proj_b)


def _ln_core(x_ref, y_ref, gt_ref, g_ref, b_ref):
    z = DN_ALPHA * x_ref[...] + gt_ref[...] * y_ref[...]
    mu = jnp.mean(z, axis=-1, keepdims=True)
    zc = z - mu
    var = jnp.mean(zc * zc, axis=-1, keepdims=True)
    return zc * lax.rsqrt(var + LN_EPS) * g_ref[...] + b_ref[...]


def _ln_kernel(x_ref, y_ref, gt_ref, g_ref, b_ref, sc_ref, sh_ref, xo_ref, uo_ref):
    xn = _ln_core(x_ref, y_ref, gt_ref, g_ref, b_ref)
    xo_ref[...] = xn
    uo_ref[...] = (xn * (1.0 + sc_ref[...]) + sh_ref[...]).astype(uo_ref.dtype)


def _route(logits):
    n_e = logits.shape[-1]
    lane = lax.broadcasted_iota(jnp.int32, logits.shape, 1)
    work = logits
    picked = jnp.zeros(logits.shape, f32)
    top = None
    for r in range(TOP_K):
        m = jnp.max(work, axis=-1, keepdims=True)
        if r == 0:
            top = m
        idx = jnp.min(jnp.where(work == m, lane, n_e), axis=-1, keepdims=True)
        hit = lane == idx
        picked = jnp.where(hit, 1.0, picked)
        work = jnp.where(hit, -jnp.inf, work)
    e = jnp.where(picked > 0.0, jnp.exp(logits - top), 0.0)
    return e / jnp.sum(e, axis=-1, keepdims=True)


def _ln_route_kernel(x_ref, y_ref, gt_ref, g_ref, b_ref, sc_ref, sh_ref, wr_ref, br_ref,
                     xo_ref, uo_ref, comb_ref):
    xn = _ln_core(x_ref, y_ref, gt_ref, g_ref, b_ref)
    xo_ref[...] = xn
    u = xn * (1.0 + sc_ref[...]) + sh_ref[...]
    uo_ref[...] = u.astype(uo_ref.dtype)
    logits = _dot3(u, wr_ref[...]) + br_ref[...]
    comb_ref[...] = _route(logits)


def _ln(x, y, gt, g, b, sc, sh, layer, router=None):
    s, d = x.shape
    tm = min(256, s)
    tile = pl.BlockSpec((tm, d), lambda i: (i, 0))
    row = pl.BlockSpec((1, d), lambda i: (0, 0))
    lrow = pl.BlockSpec((None, 1, d), lambda i: (layer, 0, 0))
    in_specs = [tile, tile, row, lrow, lrow, row, row]
    args = [x, y, gt, g, b, sc, sh]
    out_specs = [tile, tile]
    out_shape = [jax.ShapeDtypeStruct((s, d), f32), jax.ShapeDtypeStruct((s, d), bf16)]
    if router is None:
        body, name = _ln_kernel, "residual_ln"
    else:
        body, name = _ln_route_kernel, "residual_ln_route"
        w_router, b_router = router
        in_specs += [pl.BlockSpec((None, d, N_EXPERTS), lambda i: (layer, 0, 0)),
                     pl.BlockSpec((None, 1, N_EXPERTS), lambda i: (layer, 0, 0))]
        args += [w_router, b_router]
        out_specs.append(pl.BlockSpec((tm, N_EXPERTS), lambda i: (i, 0)))
        out_shape.append(jax.ShapeDtypeStruct((s, N_EXPERTS), f32))
    return pl.pallas_call(
        body,
        grid=(s // tm,),
        in_specs=in_specs,
        out_specs=out_specs,
        out_shape=out_shape,
        compiler_params=_params(("parallel",)),
        name=name,
    )(*args)


def _moe_kernel(u_ref, comb_ref, wgu_ref, bgu_ref, wd_ref, bd_ref, o_ref):
    e = pl.program_id(1)
    comb = comb_ref[...]

    @pl.when(e == 0)
    def _():
        o_ref[...] = _dot3(comb, bd_ref[...])

    gu = _dot(u_ref[...], wgu_ref[...]) + bgu_ref[...]
    gate = jnp.minimum(gu[:, :D_EXPERT], SWIGLU_LIMIT)
    up = jnp.clip(gu[:, D_EXPERT:], -SWIGLU_LIMIT, SWIGLU_LIMIT)
    h = (up + 1.0) * gate * jax.nn.sigmoid(SWIGLU_ALPHA * gate)
    lane = lax.broadcasted_iota(jnp.int32, comb.shape, 1)
    c = jnp.sum(jnp.where(lane == e, comb, 0.0), axis=1, keepdims=True)
    o_ref[...] += _dot((h * c).astype(bf16), wd_ref[...])


def _moe(u, comb, w_gate_up, b_gate_up, w_down, b_down, layer):
    s, d = u.shape
    tm = min(512, s)
    return pl.pallas_call(
        _moe_kernel,
        grid=(s // tm, N_EXPERTS),
        in_specs=[pl.BlockSpec((tm, d), lambda i, e: (i, 0)),
                  pl.BlockSpec((tm, N_EXPERTS), lambda i, e: (i, 0)),
                  pl.BlockSpec((None, None, d, 2 * D_EXPERT), lambda i, e: (layer, e, 0, 0)),
                  pl.BlockSpec((None, None, 1, 2 * D_EXPERT), lambda i, e: (layer, e, 0, 0)),
                  pl.BlockSpec((None, None, D_EXPERT, d), lambda i, e: (layer, e, 0, 0)),
                  pl.BlockSpec((None, N_EXPERTS, d), lambda i, e: (layer, 0, 0))],
        out_specs=pl.BlockSpec((tm, d), lambda i, e: (i, 0)),
        out_shape=jax.ShapeDtypeStruct((s, d), f32),
        compiler_params=_params(("parallel", "arbitrary")),
        name="moe_experts",
    )(u, comb, w_gate_up, b_gate_up, w_down, b_down)


def _silu(x):
    return x * jax.nn.sigmoid(x)


def _l2norm(x):
    return x * lax.rsqrt(jnp.sum(x * x, axis=-1, keepdims=True) + NORM_EPS)


def _ba_kernel(u_ref, w_ref, alog_ref, dt_ref, o_ref):
    tm = u_ref.shape[0]
    ba = _dot(u_ref[...], w_ref[...])
    o_ref[:, :LANES] = jax.nn.sigmoid(ba[:, :LANES])
    xa = ba[:, LANES:] + dt_ref[...]
    softplus = jnp.maximum(xa, 0.0) + jnp.log(1.0 + jnp.exp(-jnp.abs(xa)))
    g = -jnp.exp(alog_ref[...]) * softplus
    row_i = lax.broadcasted_iota(jnp.int32, (CHUNK, CHUNK), 0)
    col_i = lax.broadcasted_iota(jnp.int32, (CHUNK, CHUNK), 1)
    tril = jnp.where(row_i >= col_i, 1.0, 0.0).astype(bf16)
    for c in range(tm // CHUNK):
        r = slice(c * CHUNK, (c + 1) * CHUNK)
        g1 = g[r].astype(bf16)
        r1 = g[r] - g1.astype(f32)
        g2 = r1.astype(bf16)
        g3 = (r1 - g2.astype(f32)).astype(bf16)
        o_ref[r, LANES:] = _dot(tril, g1) + _dot(tril, g2) + _dot(tril, g3)


def _ba(u, w_ba, alog, dtb, layer):
    s, d = u.shape
    tm = min(1024, s)
    lrow = pl.BlockSpec((None, 1, LANES), lambda i: (layer, 0, 0))
    return pl.pallas_call(
        _ba_kernel,
        grid=(s // tm,),
        in_specs=[pl.BlockSpec((tm, d), lambda i: (i, 0)),
                  pl.BlockSpec((None, d, 2 * LANES), lambda i: (layer, 0, 0)),
                  lrow, lrow],
        out_specs=pl.BlockSpec((tm, 2 * LANES), lambda i: (i, 0)),
        out_shape=jax.ShapeDtypeStruct((s, 2 * LANES), f32),
        compiler_params=_params(("parallel",)),
        name="beta_decay",
    )(u, w_ba, alog, dtb)


def _delta_kernel(heads,
                  q_ref, k_ref, v_ref, z_ref, qh_ref, kh_ref, vh_ref, beta_ref, cum_ref,
                  cq_ref, ck_ref, cv_ref, na_ref,
                  o_ref, state_ref, xs_ref, cumt_ref):
    hp = pl.program_id(0)
    i = pl.program_id(1)

    @pl.when(i == 0)
    def _():
        state_ref[...] = jnp.zeros_like(state_ref)

    first = i == 0
    row_i = lax.broadcasted_iota(jnp.int32, (CHUNK, CHUNK), 0)
    col_i = lax.broadcasted_iota(jnp.int32, (CHUNK, CHUNK), 1)
    causal = row_i >= col_i
    strict = row_i > col_i
    eye = jnp.where(row_i == col_i, 1.0, 0.0)
    merge_masks = []
    for shift in range(int(math.log2(CHUNK))):
        rb = row_i >> shift
        merge_masks.append(jnp.logical_and((rb & 1) == 1, (col_i >> shift) == rb - 1))

    def conv_silu(x_ref, halo_ref, cw_ref):
        xs_ref[0:8, :] = jnp.where(first, 0.0, halo_ref[...])
        xs_ref[8:8 + CHUNK, :] = x_ref[...]
        acc = cw_ref[CONV_K - 1:CONV_K, :] * xs_ref[8:8 + CHUNK, :]
        for back in range(1, CONV_K):
            tap = CONV_K - 1 - back
            acc = acc + cw_ref[tap:tap + 1, :] * xs_ref[8 - back:8 - back + CHUNK, :]
        return _silu(acc)

    q_all = conv_silu(q_ref, qh_ref, cq_ref)
    k_all = conv_silu(k_ref, kh_ref, ck_ref)
    v_all = conv_silu(v_ref, vh_ref, cv_ref)
    beta_t = beta_ref[...]
    cum_tile = cum_ref[...]
    cumt_ref[...] = cum_tile.T

    hs = range(heads)
    sls = [slice(j * LANES, (j + 1) * LANES) for j in hs]
    q, k, v, beta_b, cum, decay = [], [], [], [], [], []
    for j in hs:
        head = hp * heads + j
        q.append(_l2norm(q_all[:, sls[j]]) * (DKA ** -0.5))
        k.append(_l2norm(k_all[:, sls[j]]))
        v.append(v_all[:, sls[j]])
        pick = col_i == head
        beta_b.append(jnp.broadcast_to(
            jnp.sum(jnp.where(pick, beta_t, 0.0), axis=1, keepdims=True), (CHUNK, CHUNK)))
        cum.append(jnp.broadcast_to(
            jnp.sum(jnp.where(pick, cum_tile, 0.0), axis=1, keepdims=True), (CHUNK, CHUNK)))
        cum_t = jnp.broadcast_to(cumt_ref[pl.ds(head, 1), :], (CHUNK, CHUNK))
        decay.append(jnp.exp(jnp.where(causal, cum[j] - cum_t, NEG_BIG)))
    k16 = [k[j].astype(bf16) for j in hs]
    kb = [k[j] * beta_b[j] for j in hs]
    lmat = [jnp.where(strict, _dot_nt(kb[j].astype(bf16), k16[j]) * decay[j], 0.0) for j in hs]
    intra = [jnp.where(causal, _dot_nt(q[j].astype(bf16), k16[j]) * decay[j], 0.0).astype(bf16)
             for j in hs]
    tn = [jnp.where(merge_masks[0], -lmat[j], 0.0) for j in hs]
    for mask in merge_masks[1:]:
        t_inv = [(eye + tn[j]).astype(bf16) for j in hs]
        half = [_dot(t_inv[j], jnp.where(mask, lmat[j], 0.0).astype(bf16)).astype(bf16) for j in hs]
        tn = [tn[j] - _dot(half[j], t_inv[j]) for j in hs]
    e_g = [jnp.exp(cum[j]) for j in hs]
    g_last = [cum[j][CHUNK - 1:CHUNK, :] for j in hs]
    rhs = [jnp.concatenate([v[j] * beta_b[j], kb[j] * e_g[j]], axis=1) for j in hs]
    uw = [rhs[j] + _dot(tn[j].astype(bf16), rhs[j].astype(bf16)) for j in hs]
    k_dec_t = [(k[j] * jnp.exp(g_last[j] - cum[j])).T.astype(bf16) for j in hs]
    state = [state_ref[j] for j in hs]
    ws = [_dot(jnp.concatenate([uw[j][:, DVA:], q[j] * e_g[j]], axis=0).astype(bf16),
               state[j].astype(bf16)) for j in hs]
    v16 = [(uw[j][:, :DVA] - ws[j][:CHUNK]).astype(bf16) for j in hs]
    o_c = [ws[j][CHUNK:] + _dot(intra[j], v16[j]) for j in hs]
    for j in hs:
        state_ref[j] = state[j] * jnp.exp(g_last[j]) + _dot(k_dec_t[j], v16[j])
    for j in hs:
        o_n = o_c[j] * lax.rsqrt(jnp.mean(o_c[j] * o_c[j], axis=-1, keepdims=True) + NORM_EPS)
        o_ref[:, sls[j]] = (o_n * na_ref[...] * _silu(z_ref[:, sls[j]])).astype(o_ref.dtype)


def _delta(proj_a, bg, conv_w, norm_a, layer):
    s = proj_a.shape[0]
    heads = 8
    hw = heads * LANES
    halo_blocks = CHUNK // 8

    def tile(col0):
        return pl.BlockSpec((CHUNK, hw), lambda hp, i: (i, hp + col0 // hw))

    def halo(col0):
        return pl.BlockSpec(
            (8, hw), lambda hp, i: (jnp.maximum(i * halo_blocks - 1, 0), hp + col0 // hw))

    def cw(col0):
        return pl.BlockSpec((None, CONV_K, hw), lambda hp, i: (layer, 0, hp + col0 // hw))

    return pl.pallas_call(
        functools.partial(_delta_kernel, heads),
        grid=(WA // hw, s // CHUNK),
        in_specs=[tile(0), tile(WA), tile(2 * WA), tile(2 * WA + WVA),
                  halo(0), halo(WA), halo(2 * WA),
                  pl.BlockSpec((CHUNK, LANES), lambda hp, i: (i, 0)),
                  pl.BlockSpec((CHUNK, LANES), lambda hp, i: (i, 1)),
                  cw(0), cw(WA), cw(2 * WA),
                  pl.BlockSpec((None, 1, LANES), lambda hp, i: (layer, 0, 0))],
        out_specs=pl.BlockSpec((CHUNK, hw), lambda hp, i: (i, hp)),
        out_shape=jax.ShapeDtypeStruct((s, WVA), bf16),
        scratch_shapes=[pltpu.VMEM((heads, DKA, DVA), f32),
                        pltpu.VMEM((CHUNK + 8, hw), f32),
                        pltpu.VMEM((LANES, CHUNK), f32)],
        compiler_params=_params(("parallel", "arbitrary")),
        name="gated_delta",
    )(proj_a, proj_a, proj_a, proj_a, proj_a, proj_a, proj_a, bg, bg,
      conv_w, conv_w, conv_w, norm_a)


def _attn_kernel(layer, q_ref, kp_ref, kc_ref, vp_ref, vc_ref, bm_ref, sink_ref, o_ref):
    n = pl.program_id(0)
    lane = lax.broadcasted_iota(jnp.int32, (2 * BLK, LANES), 1)
    low = lane < DHB
    key_j = lax.broadcasted_iota(jnp.int32, (BLK, 2 * BLK), 1)
    no_prev = jnp.logical_and(n == 0, key_j < BLK)
    scale = DHB ** -0.5
    groups_per_tile = LANES // DHB
    q_per_kv = HB // HB_KV
    for t in range(WKVB // LANES):
        sl = slice(t * LANES, (t + 1) * LANES)
        k_t = jnp.concatenate([kp_ref[:, sl], kc_ref[:, sl]], axis=0)
        v_t = jnp.concatenate([vp_ref[:, sl], vc_ref[:, sl]], axis=0)
        for gg in range(groups_per_tile):
            grp = t * groups_per_tile + gg
            mine = low if gg == 0 else jnp.logical_not(low)
            k_m = jnp.where(mine, k_t, 0.0)
            v_m = jnp.where(mine, v_t, 0.0)
            k_2 = k_m + pltpu.roll(k_m, DHB, axis=1)
            v_2 = v_m + pltpu.roll(v_m, DHB, axis=1)
            k_cat = jnp.concatenate([jnp.where(low, k_2, 0.0),
                                     jnp.where(low, 0.0, k_2)], axis=0).astype(bf16)
            v_cat = jnp.concatenate([jnp.where(low, v_2, 0.0),
                                     jnp.where(low, 0.0, v_2)], axis=0).astype(bf16)
            for pp in range(q_per_kv // 2):
                pair = grp * (q_per_kv // 2) + pp
                ql = slice(pair * LANES, (pair + 1) * LANES)
                s = _dot_nt(q_ref[:, ql].astype(bf16), k_cat) * scale
                probs = []
                for hh in range(2):
                    head = 2 * pair + hh
                    sh = s[:, hh * 2 * BLK:(hh + 1) * 2 * BLK] + bm_ref[head]
                    sh = jnp.where(no_prev, NEG_BIG, sh)
                    sink = sink_ref[layer, head]
                    mx = jnp.maximum(jnp.max(sh, axis=-1, keepdims=True), sink)
                    e = jnp.exp(sh - mx)
                    den = jnp.sum(e, axis=-1, keepdims=True) + jnp.exp(sink - mx)
                    probs.append((e * (1.0 / den)).astype(bf16))
                o_ref[:, ql] = _dot(jnp.concatenate(probs, axis=1), v_cat).astype(o_ref.dtype)


def _attn(proj_b, bias_mask, sinks, layer):
    s = proj_b.shape[0]
    kb0 = WQB // WKVB
    return pl.pallas_call(
        functools.partial(_attn_kernel, layer),
        grid=(s // BLK,),
        in_specs=[pl.BlockSpec((BLK, WQB), lambda n: (n, 0)),
                  pl.BlockSpec((BLK, WKVB), lambda n: (jnp.maximum(n - 1, 0), kb0)),
                  pl.BlockSpec((BLK, WKVB), lambda n: (n, kb0)),
                  pl.BlockSpec((BLK, WKVB), lambda n: (jnp.maximum(n - 1, 0), kb0 + 1)),
                  pl.BlockSpec((BLK, WKVB), lambda n: (n, kb0 + 1)),
                  pl.BlockSpec((HB, BLK, 2 * BLK), lambda n: (0, 0, 0)),
                  pl.BlockSpec(memory_space=pltpu.SMEM)],
        out_specs=pl.BlockSpec((BLK, WQB), lambda n: (n, 0)),
        out_shape=jax.ShapeDtypeStruct((s, WQB), bf16),
        compiler_params=_params(("parallel",)),
        name="swa_attention",
    )(proj_b, proj_b, proj_b, proj_b, proj_b, bias_mask, sinks)


def _t5_bucket(d):
    n_exact = N_BUCKETS // 2
    df = jnp.maximum(d, 1).astype(f32)
    large = n_exact + (jnp.log(df / n_exact) / math.log(MAX_DIST / n_exact)
                       * (N_BUCKETS - n_exact)).astype(jnp.int32)
    large = jnp.minimum(large, N_BUCKETS - 1)
    return jnp.where(d < n_exact, d, large)


def _band_bias(rel_bias):
    i = jnp.arange(BLK)[:, None]
    j = jnp.arange(2 * BLK)[None, :]
    d = i + BLK - j
    inside = (d >= 0) & (d < WINDOW)
    bucket = _t5_bucket(jnp.clip(d, 0, MAX_DIST - 1))
    bias = rel_bias.astype(f32)[bucket].transpose(2, 0, 1)
    return jnp.where(inside[None], bias, NEG_BIG)


def kernel(x, c, w_ada, b_ada, ada_table, rel_bias, w_in, conv_w, a_log, dt_bias, norm_a, sinks,
           w_up_a, w_up_b, w_o, ln1_g, ln1_b, w_router, b_router, w_gate_up, b_gate_up, w_down,
           b_down, ln2_g, ln2_b):
    bsz, s, d = x.shape
    assert bsz == 1 and d == D_MODEL and s % BLK == 0
    depth = w_in.shape[0]
    xs = x.reshape(s, d)

    mod_base = _ada(jnp.broadcast_to(c, (8, d)), w_ada, b_ada)[0]
    mod = mod_base.reshape(1, N_MOD, d) + ada_table

    w_a = w_in[:, :, :W_PROJ_A].astype(bf16)
    w_b = w_in[:, :, OFF_B:].astype(bf16)
    w_ba = w_in[:, :, OFF_BA:OFF_B]
    w_ba = jnp.concatenate(
        [jnp.pad(w_ba[:, :, :HA], ((0, 0), (0, 0), (0, LANES - HA))),
         jnp.pad(w_ba[:, :, HA:], ((0, 0), (0, 0), (0, LANES - HA)))], axis=-1).astype(bf16)
    w_ua = w_up_a.astype(bf16)
    w_ub = w_up_b.astype(bf16)
    w_o16 = w_o.astype(bf16)
    w_gu = w_gate_up.astype(bf16)
    w_dn = w_down.astype(bf16)
    b_gu = b_gate_up.reshape(depth, N_EXPERTS, 1, 2 * D_EXPERT)
    b_rt = b_router.reshape(depth, 1, N_EXPERTS)
    pad_h = ((0, 0), (0, LANES - HA))
    alog = jnp.pad(a_log, pad_h).reshape(depth, 1, LANES)
    dtb = jnp.pad(dt_bias, pad_h).reshape(depth, 1, LANES)
    na = norm_a.reshape(depth, 1, DVA)
    g1 = ln1_g.reshape(depth, 1, d)
    b1 = ln1_b.reshape(depth, 1, d)
    g2 = ln2_g.reshape(depth, 1, d)
    b2 = ln2_b.reshape(depth, 1, d)
    bias_mask = _band_bias(rel_bias)

    def mrow(layer, idx):
        return mod[layer, idx].reshape(1, d)

    u = _modulate(xs, mrow(0, 1), mrow(0, 0))
    for layer in range(depth):
        proj_a = _matmul(u, w_a, layer, 0, W_PROJ_A, f32, 1024, 1024, "in_proj_a")
        proj_b = _matmul(u, w_b, layer, 0, W_PROJ_B, f32, 1024, 1024, "in_proj_b")
        bg = _ba(u, w_ba, alog, dtb, layer)
        o_a = _delta(proj_a, bg, conv_w, na, layer)
        o_b = _attn(proj_b, bias_mask, sinks, layer)
        merged = _merge(o_a, o_b, w_ua, w_ub, proj_b, layer)
        y = _matmul(merged, w_o16, layer, 0, d, f32, 1024, 1024, "out_proj")
        xs, u, comb = _ln(xs, y, mrow(layer, 2), g1, b1, mrow(layer, 4), mrow(layer, 3), layer,
                          router=(w_router, b_rt))
        y = _moe(u, comb, w_gu, b_gu, w_dn, b_down, layer)
        nxt = min(layer + 1, depth - 1)
        xs, u = _ln(xs, y, mrow(layer, 5), g2, b2, mrow(nxt, 1), mrow(nxt, 0), layer)
    return xs.reshape(bsz, s, d)
```

```python
import functools
import math

import jax
import jax.numpy as jnp
import numpy as np
from jax import lax
from jax.experimental import pallas as pl
from jax.experimental.pallas import tpu as pltpu

D_MODEL = 4096
DEPTH = 4
HA = 16
DKA = 128
DVA = 128
CONV_K = 4
HB = 32
HB_KV = 8
DHB = 64
WINDOW = 128
BLK = 128
N_BUCKETS = 32
MAX_DIST = WINDOW
N_EXPERTS = 32
TOP_K = 4
D_EXPERT = 256
SWIGLU_LIMIT = 7.0
SWIGLU_ALPHA = 1.702
DN_ALPHA = (2 * DEPTH) ** 0.25
LN_EPS = 1e-5
NORM_EPS = 1e-6
N_MOD = 6
WA = HA * DKA
WVA = HA * DVA
WQB = HB * DHB
WKVB = HB_KV * DHB
OFF_BA = 2 * WA + 2 * WVA
OFF_B = OFF_BA + 2 * HA
W_PROJ_A = OFF_BA
W_PROJ_B = WQB + 2 * WKVB + 2 * D_MODEL

LANES = 128
CHUNK = 128
VMEM_LIMIT = 56 * 1024 * 1024
NEG_BIG = -1e30

bf16 = jnp.bfloat16
f32 = jnp.float32


def _dot(a, b):
    return jnp.dot(a, b, preferred_element_type=f32)


def _dot_nt(a, b):
    return lax.dot_general(a, b, (((1,), (1,)), ((), ())), preferred_element_type=f32)


def _split2(x):
    hi = x.astype(bf16)
    lo = (x - hi.astype(f32)).astype(bf16)
    return hi, lo


def _dot3(a, b):
    ah, al = _split2(a)
    bh, bl = _split2(b)
    return _dot(ah, bh) + _dot(ah, bl) + _dot(al, bh)


def _params(sem):
    return pltpu.CompilerParams(dimension_semantics=sem, vmem_limit_bytes=VMEM_LIMIT)


def _ada_kernel(c_ref, w_ref, b_ref, o_ref):
    c = c_ref[...]
    a = c * jax.nn.sigmoid(c)
    o_ref[...] = _dot3(a, w_ref[...]) + b_ref[...]


def _ada(c8, w_ada, b_ada):
    n = w_ada.shape[1]
    tn = 512
    return pl.pallas_call(
        _ada_kernel,
        grid=(n // tn,),
        in_specs=[pl.BlockSpec((8, D_MODEL), lambda j: (0, 0)),
                  pl.BlockSpec((D_MODEL, tn), lambda j: (0, j)),
                  pl.BlockSpec((1, tn), lambda j: (0, j))],
        out_specs=pl.BlockSpec((8, tn), lambda j: (0, j)),
        out_shape=jax.ShapeDtypeStruct((8, n), f32),
        compiler_params=_params(("parallel",)),
        name="ada_mod",
    )(c8, w_ada, b_ada.reshape(1, n))


def _modulate_kernel(x_ref, sc_ref, sh_ref, u_ref):
    u_ref[...] = (x_ref[...] * (1.0 + sc_ref[...]) + sh_ref[...]).astype(u_ref.dtype)


def _modulate(x, sc, sh):
    s, d = x.shape
    tm = min(256, s)
    row = pl.BlockSpec((1, d), lambda i: (0, 0))
    return pl.pallas_call(
        _modulate_kernel,
        grid=(s // tm,),
        in_specs=[pl.BlockSpec((tm, d), lambda i: (i, 0)), row, row],
        out_specs=pl.BlockSpec((tm, d), lambda i: (i, 0)),
        out_shape=jax.ShapeDtypeStruct((s, d), bf16),
        compiler_params=_params(("parallel",)),
        name="modulate",
    )(x, sc, sh)


W_SLAB = 512


def _mm_kernel(shift, a_ref, w_ref, *rest):
    if shift:
        w2_ref, o_ref, wb_ref = rest
    else:
        o_ref, wb_ref = rest

    @pl.when(pl.program_id(1) == 0)
    def _():
        k, tn = w_ref.shape
        for r0 in range(0, k, W_SLAB):
            rs = slice(r0, r0 + W_SLAB)
            if shift:
                cat = jnp.concatenate([w_ref[rs, :], w2_ref[rs, :]], axis=1)
                wb_ref[rs, :] = pltpu.roll(cat, cat.shape[1] - shift, axis=1)[:, :tn].astype(bf16)
            else:
                wb_ref[rs, :] = w_ref[rs, :].astype(bf16)

    o_ref[...] = _dot(a_ref[...], wb_ref[...]).astype(o_ref.dtype)


def _matmul(a, w, layer, col0, ncols, out_dtype, tm, tn, name):
    m, k = a.shape
    tm = min(tm, m)
    shift = col0 % LANES
    base = col0 - shift
    assert base % tn == 0 and ncols % tn == 0 and k % W_SLAB == 0
    jb = base // tn
    in_specs = [pl.BlockSpec((tm, k), lambda j, i: (i, 0)),
                pl.BlockSpec((None, k, tn), lambda j, i: (layer, 0, j + jb))]
    args = [a, w]
    if shift:
        per = tn // LANES
        in_specs.append(pl.BlockSpec((None, k, LANES), lambda j, i: (layer, 0, (j + jb + 1) * per)))
        args.append(w)
    return pl.pallas_call(
        functools.partial(_mm_kernel, shift),
        grid=(ncols // tn, m // tm),
        in_specs=in_specs,
        out_specs=pl.BlockSpec((tm, tn), lambda j, i: (i, j)),
        out_shape=jax.ShapeDtypeStruct((m, ncols), out_dtype),
        scratch_shapes=[pltpu.VMEM((k, tn), bf16)],
        compiler_params=_params(("arbitrary", "arbitrary")),
        name=name,
    )(*args)


def _merge_kernel(oa_ref, ob_ref, wa_ref, wb_ref, ga_ref, gb_ref, o_ref, wa16_ref, wb16_ref):
    @pl.when(pl.program_id(1) == 0)
    def _():
        wa16_ref[...] = wa_ref[...].astype(bf16)
        wb16_ref[...] = wb_ref[...].astype(bf16)

    ya = _dot(oa_ref[...], wa16_ref[...])
    yb = _dot(ob_ref[...], wb16_ref[...])
    o_ref[...] = (jax.nn.sigmoid(ga_ref[...]) * ya
                  + jax.nn.sigmoid(gb_ref[...]) * yb).astype(o_ref.dtype)


def _merge(oa, ob, w_up_a, w_up_b, proj_b, layer):
    s = oa.shape[0]
    tm = min(1024, s)
    tn = 512
    ga0 = (WQB + 2 * WKVB) // tn
    gb0 = ga0 + D_MODEL // tn
    return pl.pallas_call(
        _merge_kernel,
        grid=(D_MODEL // tn, s // tm),
        in_specs=[pl.BlockSpec((tm, WVA), lambda j, i: (i, 0)),
                  pl.BlockSpec((tm, WQB), lambda j, i: (i, 0)),
                  pl.BlockSpec((None, WVA, tn), lambda j, i: (layer, 0, j)),
                  pl.BlockSpec((None, WQB, tn), lambda j, i: (layer, 0, j)),
                  pl.BlockSpec((tm, tn), lambda j, i: (i, j + ga0)),
                  pl.BlockSpec((tm, tn), lambda j, i: (i, j + gb0))],
        out_specs=pl.BlockSpec((tm, tn), lambda j, i: (i, j)),
        out_shape=jax.ShapeDtypeStruct((s, D_MODEL), bf16),
        scratch_shapes=[pltpu.VMEM((WVA, tn), bf16), pltpu.VMEM((WQB, tn), bf16)],
        compiler_params=_params(("arbitrary", "arbitrary")),
        name="gated_merge",
    )(oa, ob, w_up_a, w_up_b, proj_b, proj_b)


def _layernorm(z, g, b):
    mu = jnp.mean(z, axis=-1, keepdims=True)
    zc = z - mu
    var = jnp.mean(zc * zc, axis=-1, keepdims=True)
    return zc * lax.rsqrt(var + LN_EPS) * g + b


def _route(logits):
    tm, n_e = logits.shape
    lane = lax.broadcasted_iota(jnp.int32, logits.shape, 1)
    slot = lax.broadcasted_iota(jnp.int32, (tm, LANES), 1)
    work = logits
    ids, vals = [], []
    for _ in range(TOP_K):
        m = jnp.max(work, axis=-1, keepdims=True)
        idx = jnp.min(jnp.where(work == m, lane, n_e), axis=-1, keepdims=True)
        work = jnp.where(lane == idx, -jnp.inf, work)
        ids.append(idx)
        vals.append(m)
    es = [jnp.exp(v - vals[0]) for v in vals]
    inv = 1.0 / functools.reduce(lambda p, q: p + q, es)
    comb = jnp.zeros(logits.shape, f32)
    id_tile = jnp.zeros((tm, LANES), jnp.int32)
    w_tile = jnp.zeros((tm, LANES), f32)
    for r in range(TOP_K):
        w = es[r] * inv
        comb = jnp.where(lane == ids[r], w, comb)
        id_tile = jnp.where(slot == r, ids[r], id_tile)
        w_tile = jnp.where(slot == r, w, w_tile)
    return comb, id_tile, w_tile


def _ln_route_kernel(x_ref, y_ref, gt_ref, g_ref, b_ref, sc_ref, sh_ref, wr_ref, br_ref,
                     xo_ref, uo_ref, comb_ref, id_ref, w_ref):
    z = DN_ALPHA * x_ref[...] + gt_ref[...] * y_ref[...]
    xn = _layernorm(z, g_ref[...], b_ref[...])
    xo_ref[...] = xn
    u = xn * (1.0 + sc_ref[...]) + sh_ref[...]
    uo_ref[...] = u
    logits = _dot3(u, wr_ref[...]) + br_ref[...]
    comb_ref[...], id_ref[...], w_ref[...] = _route(logits)


def _ln_route(x, y, gt, g, b, sc, sh, w_router, b_router, layer):
    s, d = x.shape
    tm = min(256, s)
    tile = pl.BlockSpec((tm, d), lambda i: (i, 0))
    row = pl.BlockSpec((1, d), lambda i: (0, 0))
    lrow = pl.BlockSpec((None, 1, d), lambda i: (layer, 0, 0))
    small = pl.BlockSpec((tm, LANES), lambda i: (i, 0))
    return pl.pallas_call(
        _ln_route_kernel,
        grid=(s // tm,),
        in_specs=[tile, tile, row, lrow, lrow, row, row,
                  pl.BlockSpec((None, d, N_EXPERTS), lambda i: (layer, 0, 0)),
                  pl.BlockSpec((None, 1, N_EXPERTS), lambda i: (layer, 0, 0))],
        out_specs=[tile, tile, pl.BlockSpec((tm, N_EXPERTS), lambda i: (i, 0)), small, small],
        out_shape=[jax.ShapeDtypeStruct((s, d), f32), jax.ShapeDtypeStruct((s, d), f32),
                   jax.ShapeDtypeStruct((s, N_EXPERTS), f32),
                   jax.ShapeDtypeStruct((s, LANES), jnp.int32),
                   jax.ShapeDtypeStruct((s, LANES), f32)],
        compiler_params=_params(("parallel",)),
        name="residual_ln_route",
    )(x, y, gt, g, b, sc, sh, w_router, b_router)


MOE_TM = 256
DISP_TM = 256
COMB_TM = 128
PAD_BITS = tuple(1 << p for p in reversed(range(int(math.log2(MOE_TM)))))


def _plan(id_tile, s):
    idx = id_tile[:, :TOP_K]
    experts = jnp.arange(N_EXPERTS, dtype=jnp.int32)
    onehot = idx[:, :, None] == experts[None, None, :]
    mask = jnp.any(onehot, axis=1).astype(jnp.int32)
    incl = jnp.cumsum(mask, axis=0)
    counts = incl[-1]
    padded = (counts + MOE_TM - 1) // MOE_TM * MOE_TM
    ends = jnp.cumsum(padded)
    starts = ends - padded
    dest_all = starts[None, :] + incl - mask
    dest = jnp.sum(jnp.where(onehot, dest_all[:, None, :], 0), axis=-1)
    n_tiles = (TOP_K * s) // MOE_TM + N_EXPERTS
    end_tiles = ends // MOE_TM
    n_used = end_tiles[-1]
    tiles = jnp.arange(n_tiles, dtype=jnp.int32)
    tile_expert = jnp.sum((jnp.minimum(tiles, n_used - 1)[:, None] >= end_tiles[None, :])
                          .astype(jnp.int32), axis=1)
    tile_expert = jnp.minimum(tile_expert, N_EXPERTS - 1)
    return (dest.astype(jnp.int32), tile_expert.astype(jnp.int32),
            n_used.reshape(1).astype(jnp.int32), (starts + counts).astype(jnp.int32),
            (padded - counts).astype(jnp.int32), n_tiles)


def _dispatch_kernel(pad_row_ref, pad_len_ref, used_ref, dest_ref, u_ref, us_ref, zero_ref, sem, zsem):
    i = pl.program_id(0)
    tm = u_ref.shape[0]

    def issue(r, carry):
        for k in range(TOP_K):
            row = dest_ref[0, r * TOP_K + k]
            pltpu.make_async_copy(u_ref.at[pl.ds(r, 1)], us_ref.at[pl.ds(row, 1)], sem).start()
        return carry

    lax.fori_loop(0, tm, issue, 0)

    @pl.when(i == pl.num_programs(0) - 1)
    def _():
        zero_ref[...] = jnp.zeros_like(zero_ref)

        def fill(wait, e, carry):
            row = pad_row_ref[e]
            n = pad_len_ref[e]
            for bit in reversed(PAD_BITS):
                size = bit if bit >= 8 else 1
                for part in range(bit // size):
                    @pl.when((n & bit) != 0)
                    def _():
                        at = row + part * size
                        if size >= 8:
                            at = pl.multiple_of(at, 8)
                        cp = pltpu.make_async_copy(zero_ref.at[pl.ds(0, size)],
                                                   us_ref.at[pl.ds(at, size)], zsem)
                        if wait:
                            cp.wait()
                        else:
                            cp.start()
                row = row + (n & bit)
            return carry

        def tail(wait, t, carry):
            tile = used_ref[0] + t

            @pl.when(tile < us_ref.shape[0] // MOE_TM)
            def _():
                for part in range(MOE_TM // zero_ref.shape[0]):
                    at = pl.multiple_of(tile * MOE_TM + part * zero_ref.shape[0], 8)
                    cp = pltpu.make_async_copy(
                        zero_ref, us_ref.at[pl.ds(at, zero_ref.shape[0])], zsem)
                    if wait:
                        cp.wait()
                    else:
                        cp.start()
            return carry

        for wait in (False, True):
            lax.fori_loop(0, N_EXPERTS, functools.partial(fill, wait), 0)
            lax.fori_loop(0, N_EXPERTS, functools.partial(tail, wait), 0)

    for _ in range(TOP_K):
        pltpu.make_async_copy(u_ref, us_ref.at[pl.ds(0, tm)], sem).wait()


def _dispatch(u, dest, pad_row, pad_len, n_used, n_tiles):
    s, d = u.shape
    tm = min(DISP_TM, s)
    dest3 = dest.reshape(s // tm, 1, tm * TOP_K)
    return pl.pallas_call(
        _dispatch_kernel,
        grid_spec=pltpu.PrefetchScalarGridSpec(
            num_scalar_prefetch=3,
            grid=(s // tm,),
            in_specs=[pl.BlockSpec((None, 1, tm * TOP_K), lambda i, *_: (i, 0, 0),
                                   memory_space=pltpu.SMEM),
                      pl.BlockSpec((tm, d), lambda i, *_: (i, 0))],
            out_specs=pl.BlockSpec(memory_space=pl.ANY),
            scratch_shapes=[pltpu.VMEM((MOE_TM // 2, d), f32),
                            pltpu.SemaphoreType.DMA(()), pltpu.SemaphoreType.DMA(())]),
        out_shape=jax.ShapeDtypeStruct((n_tiles * MOE_TM, d), f32),
        compiler_params=_params(("arbitrary",)),
        name="moe_dispatch",
    )(pad_row, pad_len, n_used, dest3, u)


def _experts_kernel(te_ref, nu_ref, x_ref, wgu_ref, bgu_ref, wd_ref, y_ref, wgu16_ref, wd16_ref):
    i = pl.program_id(0)
    new_expert = jnp.logical_or(i == 0, te_ref[i] != te_ref[jnp.maximum(i - 1, 0)])

    @pl.when(new_expert)
    def _():
        for r0 in range(0, wgu_ref.shape[0], W_SLAB):
            rs = slice(r0, r0 + W_SLAB)
            wgu16_ref[rs, :] = wgu_ref[rs, :].astype(bf16)
        wd16_ref[...] = wd_ref[...].astype(bf16)

    @pl.when(i < nu_ref[0])
    def _():
        gu = _dot(x_ref[...].astype(bf16), wgu16_ref[...]) + bgu_ref[...]
        gate = jnp.minimum(gu[:, :D_EXPERT], SWIGLU_LIMIT)
        up = jnp.clip(gu[:, D_EXPERT:], -SWIGLU_LIMIT, SWIGLU_LIMIT)
        h = (up + 1.0) * gate * jax.nn.sigmoid(SWIGLU_ALPHA * gate)
        y_ref[...] = _dot(h.astype(bf16), wd16_ref[...])

    @pl.when(i >= nu_ref[0])
    def _():
        y_ref[...] = jnp.zeros_like(y_ref)


def _experts(us, tile_expert, n_used, w_gate_up, b_gate_up, w_down, layer):
    rows, d = us.shape
    return pl.pallas_call(
        _experts_kernel,
        grid_spec=pltpu.PrefetchScalarGridSpec(
            num_scalar_prefetch=2,
            grid=(rows // MOE_TM,),
            in_specs=[pl.BlockSpec((MOE_TM, d), lambda i, te, nu: (jnp.minimum(i, nu[0] - 1), 0)),
                      pl.BlockSpec((None, None, d, 2 * D_EXPERT),
                                   lambda i, te, nu: (layer, te[i], 0, 0)),
                      pl.BlockSpec((None, None, 1, 2 * D_EXPERT),
                                   lambda i, te, nu: (layer, te[i], 0, 0)),
                      pl.BlockSpec((None, None, D_EXPERT, d),
                                   lambda i, te, nu: (layer, te[i], 0, 0))],
            out_specs=pl.BlockSpec((MOE_TM, d), lambda i, te, nu: (i, 0)),
            scratch_shapes=[pltpu.VMEM((d, 2 * D_EXPERT), bf16),
                            pltpu.VMEM((D_EXPERT, d), bf16)]),
        out_shape=jax.ShapeDtypeStruct((rows, d), f32),
        compiler_params=_params(("arbitrary",)),
        name="moe_experts",
    )(tile_expert, n_used, us, w_gate_up, b_gate_up, w_down)


def _combine_ln_kernel(dest_ref, dnext_ref, x_ref, w_ref, comb_ref, bd_ref, gt_ref, g_ref, b_ref,
                       sc_ref, sh_ref, ys_ref, xo_ref, uo_ref, buf_ref, sem):
    i = pl.program_id(0)
    n = pl.num_programs(0)
    tm = x_ref.shape[0]
    slot = i % 2

    def gather(d_ref, to):
        def body(r, carry):
            for k in range(TOP_K):
                row = d_ref[0, r * TOP_K + k]
                pltpu.make_async_copy(ys_ref.at[pl.ds(row, 1)],
                                      buf_ref.at[to, k, pl.ds(r, 1)], sem.at[to]).start()
            return carry
        lax.fori_loop(0, tm, body, 0)

    @pl.when(i == 0)
    def _():
        gather(dest_ref, 0)

    @pl.when(i + 1 < n)
    def _():
        gather(dnext_ref, 1 - slot)

    for k in range(TOP_K):
        pltpu.make_async_copy(ys_ref.at[pl.ds(0, tm)], buf_ref.at[slot, k], sem.at[slot]).wait()

    w_tile = w_ref[...]
    lane = lax.broadcasted_iota(jnp.int32, w_tile.shape, 1)
    y = _dot3(comb_ref[...], bd_ref[...])
    for k in range(TOP_K):
        w_k = jnp.sum(jnp.where(lane == k, w_tile, 0.0), axis=1, keepdims=True)
        y = y + w_k * buf_ref[slot, k]
    z = DN_ALPHA * x_ref[...] + gt_ref[...] * y
    xn = _layernorm(z, g_ref[...], b_ref[...])
    xo_ref[...] = xn
    uo_ref[...] = (xn * (1.0 + sc_ref[...]) + sh_ref[...]).astype(uo_ref.dtype)


def _combine_ln(x, ys, dest, w_tile, comb, b_down, gt, g, b, sc, sh, layer):
    s, d = x.shape
    tm = min(COMB_TM, s)
    steps = s // tm
    dest3 = dest.reshape(steps, 1, tm * TOP_K)
    tile = pl.BlockSpec((tm, d), lambda i: (i, 0))
    row = pl.BlockSpec((1, d), lambda i: (0, 0))
    lrow = pl.BlockSpec((None, 1, d), lambda i: (layer, 0, 0))
    return pl.pallas_call(
        _combine_ln_kernel,
        grid=(steps,),
        in_specs=[pl.BlockSpec((None, 1, tm * TOP_K), lambda i: (i, 0, 0), memory_space=pltpu.SMEM),
                  pl.BlockSpec((None, 1, tm * TOP_K),
                               lambda i: (jnp.minimum(i + 1, steps - 1), 0, 0),
                               memory_space=pltpu.SMEM),
                  tile,
                  pl.BlockSpec((tm, LANES), lambda i: (i, 0)),
                  pl.BlockSpec((tm, N_EXPERTS), lambda i: (i, 0)),
                  pl.BlockSpec((None, N_EXPERTS, d), lambda i: (layer, 0, 0)),
                  row, lrow, lrow, row, row,
                  pl.BlockSpec(memory_space=pl.ANY)],
        out_specs=[tile, tile],
        out_shape=[jax.ShapeDtypeStruct((s, d), f32), jax.ShapeDtypeStruct((s, d), bf16)],
        scratch_shapes=[pltpu.VMEM((2, TOP_K, tm, d), f32), pltpu.SemaphoreType.DMA((2,))],
        compiler_params=_params(("arbitrary",)),
        name="moe_combine_ln",
    )(dest3, dest3, x, w_tile, comb, b_down, gt, g, b, sc, sh, ys)


def _silu(x):
    return x * jax.nn.sigmoid(x)


def _l2norm(x):
    return x * lax.rsqrt(jnp.sum(x * x, axis=-1, keepdims=True) + NORM_EPS)


def _ba_kernel(u_ref, w_ref, alog_ref, dt_ref, o_ref):
    tm = u_ref.shape[0]
    ba = _dot(u_ref[...], w_ref[...].astype(bf16))
    o_ref[:, :LANES] = jax.nn.sigmoid(ba)
    xa = ba + dt_ref[...]
    softplus = jnp.maximum(xa, 0.0) + jnp.log(1.0 + jnp.exp(-jnp.abs(xa)))
    g = -jnp.exp(alog_ref[...]) * softplus
    row_i = lax.broadcasted_iota(jnp.int32, (CHUNK, CHUNK), 0)
    col_i = lax.broadcasted_iota(jnp.int32, (CHUNK, CHUNK), 1)
    tril = jnp.where(row_i >= col_i, 1.0, 0.0).astype(bf16)
    for c in range(tm // CHUNK):
        r = slice(c * CHUNK, (c + 1) * CHUNK)
        g1 = g[r].astype(bf16)
        r1 = g[r] - g1.astype(f32)
        g2 = r1.astype(bf16)
        g3 = (r1 - g2.astype(f32)).astype(bf16)
        o_ref[r, LANES:] = _dot(tril, g1) + _dot(tril, g2) + _dot(tril, g3)


def _ba(u, w_in, alog, dtb, layer):
    s, d = u.shape
    tm = min(1024, s)
    lrow = pl.BlockSpec((None, 1, LANES), lambda i: (layer, 0, 0))
    return pl.pallas_call(
        _ba_kernel,
        grid=(s // tm,),
        in_specs=[pl.BlockSpec((tm, d), lambda i: (i, 0)),
                  pl.BlockSpec((None, d, LANES), lambda i: (layer, 0, OFF_BA // LANES)),
                  lrow, lrow],
        out_specs=pl.BlockSpec((tm, 2 * LANES), lambda i: (i, 0)),
        out_shape=jax.ShapeDtypeStruct((s, 2 * LANES), f32),
        compiler_params=_params(("parallel",)),
        name="beta_decay",
    )(u, w_in, alog, dtb)


def _delta_kernel(heads,
                  q_ref, k_ref, v_ref, z_ref, qh_ref, kh_ref, vh_ref, beta_ref, cum_ref,
                  cq_ref, ck_ref, cv_ref, na_ref,
                  o_ref, state_ref, xs_ref, cumt_ref):
    hp = pl.program_id(0)
    i = pl.program_id(1)

    @pl.when(i == 0)
    def _():
        state_ref[...] = jnp.zeros_like(state_ref)

    first = i == 0
    row_i = lax.broadcasted_iota(jnp.int32, (CHUNK, CHUNK), 0)
    col_i = lax.broadcasted_iota(jnp.int32, (CHUNK, CHUNK), 1)
    causal = row_i >= col_i
    strict = row_i > col_i
    eye = jnp.where(row_i == col_i, 1.0, 0.0)
    merge_masks = []
    for shift in range(int(math.log2(CHUNK))):
        rb = row_i >> shift
        merge_masks.append(jnp.logical_and((rb & 1) == 1, (col_i >> shift) == rb - 1))

    def conv_silu(x_ref, halo_ref, cw_ref):
        xs_ref[0:8, :] = jnp.where(first, 0.0, halo_ref[...])
        xs_ref[8:8 + CHUNK, :] = x_ref[...]
        acc = cw_ref[CONV_K - 1:CONV_K, :] * xs_ref[8:8 + CHUNK, :]
        for back in range(1, CONV_K):
            tap = CONV_K - 1 - back
            acc = acc + cw_ref[tap:tap + 1, :] * xs_ref[8 - back:8 - back + CHUNK, :]
        return _silu(acc)

    q_all = conv_silu(q_ref, qh_ref, cq_ref)
    k_all = conv_silu(k_ref, kh_ref, ck_ref)
    v_all = conv_silu(v_ref, vh_ref, cv_ref)
    beta_t = beta_ref[...]
    cum_tile = cum_ref[...]
    cumt_ref[...] = cum_tile.T

    hs = range(heads)
    sls = [slice(j * LANES, (j + 1) * LANES) for j in hs]
    q, k, v, beta_b, cum, decay = [], [], [], [], [], []
    for j in hs:
        head = hp * heads + j
        q.append(_l2norm(q_all[:, sls[j]]) * (DKA ** -0.5))
        k.append(_l2norm(k_all[:, sls[j]]))
        v.append(v_all[:, sls[j]])
        beta_b.append(jnp.broadcast_to(
            jnp.sum(jnp.where(col_i == head, beta_t, 0.0), axis=1, keepdims=True), (CHUNK, CHUNK)))
        cum.append(jnp.broadcast_to(
            jnp.sum(jnp.where(col_i == head + HA, cum_tile, 0.0), axis=1, keepdims=True),
            (CHUNK, CHUNK)))
        cum_t = jnp.broadcast_to(cumt_ref[pl.ds(head + HA, 1), :], (CHUNK, CHUNK))
        decay.append(jnp.exp(jnp.where(causal, cum[j] - cum_t, NEG_BIG)))
    k16 = [k[j].astype(bf16) for j in hs]
    kb = [k[j] * beta_b[j] for j in hs]
    lmat = [jnp.where(strict, _dot_nt(kb[j].astype(bf16), k16[j]) * decay[j], 0.0) for j in hs]
    intra = [jnp.where(causal, _dot_nt(q[j].astype(bf16), k16[j]) * decay[j], 0.0).astype(bf16)
             for j in hs]
    tn = [jnp.where(merge_masks[0], -lmat[j], 0.0) for j in hs]
    for mask in merge_masks[1:]:
        t_inv = [(eye + tn[j]).astype(bf16) for j in hs]
        half = [_dot(t_inv[j], jnp.where(mask, lmat[j], 0.0).astype(bf16)).astype(bf16) for j in hs]
        tn = [tn[j] - _dot(half[j], t_inv[j]) for j in hs]
    e_g = [jnp.exp(cum[j]) for j in hs]
    g_last = [cum[j][CHUNK - 1:CHUNK, :] for j in hs]
    rhs = [jnp.concatenate([v[j] * beta_b[j], kb[j] * e_g[j]], axis=1) for j in hs]
    uw = [rhs[j] + _dot(tn[j].astype(bf16), rhs[j].astype(bf16)) for j in hs]
    k_dec_t = [(k[j] * jnp.exp(g_last[j] - cum[j])).T.astype(bf16) for j in hs]
    state = [state_ref[j] for j in hs]
    ws = [_dot(jnp.concatenate([uw[j][:, DVA:], q[j] * e_g[j]], axis=0).astype(bf16),
               state[j].astype(bf16)) for j in hs]
    v16 = [(uw[j][:, :DVA] - ws[j][:CHUNK]).astype(bf16) for j in hs]
    o_c = [ws[j][CHUNK:] + _dot(intra[j], v16[j]) for j in hs]
    for j in hs:
        state_ref[j] = state[j] * jnp.exp(g_last[j]) + _dot(k_dec_t[j], v16[j])
    for j in hs:
        o_n = o_c[j] * lax.rsqrt(jnp.mean(o_c[j] * o_c[j], axis=-1, keepdims=True) + NORM_EPS)
        o_ref[:, sls[j]] = (o_n * na_ref[...] * _silu(z_ref[:, sls[j]])).astype(o_ref.dtype)


def _delta(proj_a, bg, conv_w, norm_a, layer):
    s = proj_a.shape[0]
    heads = 8
    hw = heads * LANES
    halo_blocks = CHUNK // 8

    def tile(col0):
        return pl.BlockSpec((CHUNK, hw), lambda hp, i: (i, hp + col0 // hw))

    def halo(col0):
        return pl.BlockSpec(
            (8, hw), lambda hp, i: (jnp.maximum(i * halo_blocks - 1, 0), hp + col0 // hw))

    def cw(col0):
        return pl.BlockSpec((None, CONV_K, hw), lambda hp, i: (layer, 0, hp + col0 // hw))

    return pl.pallas_call(
        functools.partial(_delta_kernel, heads),
        grid=(WA // hw, s // CHUNK),
        in_specs=[tile(0), tile(WA), tile(2 * WA), tile(2 * WA + WVA),
                  halo(0), halo(WA), halo(2 * WA),
                  pl.BlockSpec((CHUNK, LANES), lambda hp, i: (i, 0)),
                  pl.BlockSpec((CHUNK, LANES), lambda hp, i: (i, 1)),
                  cw(0), cw(WA), cw(2 * WA),
                  pl.BlockSpec((None, 1, LANES), lambda hp, i: (layer, 0, 0))],
        out_specs=pl.BlockSpec((CHUNK, hw), lambda hp, i: (i, hp)),
        out_shape=jax.ShapeDtypeStruct((s, WVA), bf16),
        scratch_shapes=[pltpu.VMEM((heads, DKA, DVA), f32),
                        pltpu.VMEM((CHUNK + 8, hw), f32),
                        pltpu.VMEM((LANES, CHUNK), f32)],
        compiler_params=_params(("parallel", "arbitrary")),
        name="gated_delta",
    )(proj_a, proj_a, proj_a, proj_a, proj_a, proj_a, proj_a, bg, bg,
      conv_w, conv_w, conv_w, norm_a)


def _attn_kernel(layer, q_ref, kp_ref, kc_ref, vp_ref, vc_ref, bm_ref, sink_ref, o_ref):
    n = pl.program_id(0)
    lane = lax.broadcasted_iota(jnp.int32, (2 * BLK, LANES), 1)
    low = lane < DHB
    key_j = lax.broadcasted_iota(jnp.int32, (BLK, 2 * BLK), 1)
    no_prev = jnp.logical_and(n == 0, key_j < BLK)
    scale = DHB ** -0.5
    groups_per_tile = LANES // DHB
    q_per_kv = HB // HB_KV
    for t in range(WKVB // LANES):
        sl = slice(t * LANES, (t + 1) * LANES)
        k_t = jnp.concatenate([kp_ref[:, sl], kc_ref[:, sl]], axis=0)
        v_t = jnp.concatenate([vp_ref[:, sl], vc_ref[:, sl]], axis=0)
        for gg in range(groups_per_tile):
            grp = t * groups_per_tile + gg
            mine = low if gg == 0 else jnp.logical_not(low)
            k_m = jnp.where(mine, k_t, 0.0)
            v_m = jnp.where(mine, v_t, 0.0)
            k_2 = k_m + pltpu.roll(k_m, DHB, axis=1)
            v_2 = v_m + pltpu.roll(v_m, DHB, axis=1)
            k_cat = jnp.concatenate([jnp.where(low, k_2, 0.0),
                                     jnp.where(low, 0.0, k_2)], axis=0).astype(bf16)
            v_cat = jnp.concatenate([jnp.where(low, v_2, 0.0),
                                     jnp.where(low, 0.0, v_2)], axis=0).astype(bf16)
            for pp in range(q_per_kv // 2):
                pair = grp * (q_per_kv // 2) + pp
                ql = slice(pair * LANES, (pair + 1) * LANES)
                s = _dot_nt(q_ref[:, ql].astype(bf16), k_cat) * scale
                probs = []
                for hh in range(2):
                    head = 2 * pair + hh
                    sh = s[:, hh * 2 * BLK:(hh + 1) * 2 * BLK] + bm_ref[head]
                    sh = jnp.where(no_prev, NEG_BIG, sh)
                    sink = sink_ref[layer, head]
                    mx = jnp.maximum(jnp.max(sh, axis=-1, keepdims=True), sink)
                    e = jnp.exp(sh - mx)
                    den = jnp.sum(e, axis=-1, keepdims=True) + jnp.exp(sink - mx)
                    probs.append((e * (1.0 / den)).astype(bf16))
                o_ref[:, ql] = _dot(jnp.concatenate(probs, axis=1), v_cat).astype(o_ref.dtype)


def _attn(proj_b, bias_mask, sinks, layer):
    s = proj_b.shape[0]
    kb0 = WQB // WKVB
    return pl.pallas_call(
        functools.partial(_attn_kernel, layer),
        grid=(s // BLK,),
        in_specs=[pl.BlockSpec((BLK, WQB), lambda n: (n, 0)),
                  pl.BlockSpec((BLK, WKVB), lambda n: (jnp.maximum(n - 1, 0), kb0)),
                  pl.BlockSpec((BLK, WKVB), lambda n: (n, kb0)),
                  pl.BlockSpec((BLK, WKVB), lambda n: (jnp.maximum(n - 1, 0), kb0 + 1)),
                  pl.BlockSpec((BLK, WKVB), lambda n: (n, kb0 + 1)),
                  pl.BlockSpec((HB, BLK, 2 * BLK), lambda n: (0, 0, 0)),
                  pl.BlockSpec(memory_space=pltpu.SMEM)],
        out_specs=pl.BlockSpec((BLK, WQB), lambda n: (n, 0)),
        out_shape=jax.ShapeDtypeStruct((s, WQB), bf16),
        compiler_params=_params(("parallel",)),
        name="swa_attention",
    )(proj_b, proj_b, proj_b, proj_b, proj_b, bias_mask, sinks)


def _t5_bucket(d):
    n_exact = N_BUCKETS // 2
    df = jnp.maximum(d, 1).astype(f32)
    large = n_exact + (jnp.log(df / n_exact) / math.log(MAX_DIST / n_exact)
                       * (N_BUCKETS - n_exact)).astype(jnp.int32)
    large = jnp.minimum(large, N_BUCKETS - 1)
    return jnp.where(d < n_exact, d, large)


def _band_bias(rel_bias):
    i = jnp.arange(BLK)[:, None]
    j = jnp.arange(2 * BLK)[None, :]
    d = i + BLK - j
    inside = (d >= 0) & (d < WINDOW)
    bucket = _t5_bucket(jnp.clip(d, 0, MAX_DIST - 1))
    bias = rel_bias.astype(f32)[bucket].transpose(2, 0, 1)
    return jnp.where(inside[None], bias, NEG_BIG)


def kernel(x, c, w_ada, b_ada, ada_table, rel_bias, w_in, conv_w, a_log, dt_bias, norm_a, sinks,
           w_up_a, w_up_b, w_o, ln1_g, ln1_b, w_router, b_router, w_gate_up, b_gate_up, w_down,
           b_down, ln2_g, ln2_b):
    bsz, s, d = x.shape
    assert bsz == 1 and d == D_MODEL and s % BLK == 0
    depth = w_in.shape[0]
    xs = x.reshape(s, d)

    mod_base = _ada(jnp.broadcast_to(c, (8, d)), w_ada, b_ada)[0]
    mod = mod_base.reshape(1, N_MOD, d) + ada_table

    b_gu = b_gate_up.reshape(depth, N_EXPERTS, 1, 2 * D_EXPERT)
    b_rt = b_router.reshape(depth, 1, N_EXPERTS)
    pad_h = ((0, 0), (HA, LANES - 2 * HA))
    alog = jnp.pad(a_log, pad_h).reshape(depth, 1, LANES)
    dtb = jnp.pad(dt_bias, pad_h).reshape(depth, 1, LANES)
    na = norm_a.reshape(depth, 1, DVA)
    g1 = ln1_g.reshape(depth, 1, d)
    b1 = ln1_b.reshape(depth, 1, d)
    g2 = ln2_g.reshape(depth, 1, d)
    b2 = ln2_b.reshape(depth, 1, d)
    bias_mask = _band_bias(rel_bias)

    def mrow(layer, idx):
        return mod[layer, idx].reshape(1, d)

    u = _modulate(xs, mrow(0, 1), mrow(0, 0))
    for layer in range(depth):
        proj_a = _matmul(u, w_in, layer, 0, W_PROJ_A, f32, 1024, 512, "in_proj_a")
        proj_b = _matmul(u, w_in, layer, OFF_B, W_PROJ_B, f32, 1024, 512, "in_proj_b")
        bg = _ba(u, w_in, alog, dtb, layer)
        o_a = _delta(proj_a, bg, conv_w, na, layer)
        o_b = _attn(proj_b, bias_mask, sinks, layer)
        merged = _merge(o_a, o_b, w_up_a, w_up_b, proj_b, layer)
        y = _matmul(merged, w_o, layer, 0, d, f32, 1024, 512, "out_proj")
        xs, u_moe, comb, id_tile, w_tile = _ln_route(
            xs, y, mrow(layer, 2), g1, b1, mrow(layer, 4), mrow(layer, 3), w_router, b_rt, layer)
        dest, tile_expert, n_used, pad_row, pad_len, n_tiles = _plan(id_tile, s)
        us = _dispatch(u_moe, dest, pad_row, pad_len, n_used, n_tiles)
        ys = _experts(us, tile_expert, n_used, w_gate_up, b_gu, w_down, layer)
        nxt = min(layer + 1, depth - 1)
        xs, u = _combine_ln(xs, ys, dest, w_tile, comb, b_down, mrow(layer, 5), g2, b2,
                            mrow(nxt, 1), mrow(nxt, 0), layer)
    return xs.reshape(bsz, s, d)
```

```python
import functools
import math

import jax
import jax.numpy as jnp
import numpy as np
from jax import lax
from jax.experimental import pallas as pl
from jax.experimental.pallas import tpu as pltpu

D_MODEL = 4096
DEPTH = 4
HA = 16
DKA = 128
DVA = 128
CONV_K = 4
HB = 32
HB_KV = 8
DHB = 64
WINDOW = 128
BLK = 128
N_BUCKETS = 32
MAX_DIST = WINDOW
N_EXPERTS = 32
TOP_K = 4
D_EXPERT = 256
SWIGLU_LIMIT = 7.0
SWIGLU_ALPHA = 1.702
DN_ALPHA = (2 * DEPTH) ** 0.25
LN_EPS = 1e-5
NORM_EPS = 1e-6
N_MOD = 6
WA = HA * DKA
WVA = HA * DVA
WQB = HB * DHB
WKVB = HB_KV * DHB
OFF_BA = 2 * WA + 2 * WVA
OFF_B = OFF_BA + 2 * HA
W_PROJ_A = OFF_BA
W_PROJ_B = WQB + 2 * WKVB + 2 * D_MODEL

LANES = 128
CHUNK = 128
VMEM_LIMIT = 56 * 1024 * 1024
NEG_BIG = -1e30

bf16 = jnp.bfloat16
f32 = jnp.float32


def _dot(a, b):
    return jnp.dot(a, b, preferred_element_type=f32)


def _dot_nt(a, b):
    return lax.dot_general(a, b, (((1,), (1,)), ((), ())), preferred_element_type=f32)


def _split2(x):
    hi = x.astype(bf16)
    lo = (x - hi.astype(f32)).astype(bf16)
    return hi, lo


def _dot3(a, b):
    ah, al = _split2(a)
    bh, bl = _split2(b)
    return _dot(ah, bh) + _dot(ah, bl) + _dot(al, bh)


def _params(sem):
    return pltpu.CompilerParams(dimension_semantics=sem, vmem_limit_bytes=VMEM_LIMIT)


def _ada_kernel(c_ref, w_ref, b_ref, o_ref):
    c = c_ref[...]
    a = c * jax.nn.sigmoid(c)
    o_ref[...] = _dot3(a, w_ref[...]) + b_ref[...]


def _ada(c8, w_ada, b_ada):
    n = w_ada.shape[1]
    tn = 512
    return pl.pallas_call(
        _ada_kernel,
        grid=(n // tn,),
        in_specs=[pl.BlockSpec((8, D_MODEL), lambda j: (0, 0)),
                  pl.BlockSpec((D_MODEL, tn), lambda j: (0, j)),
                  pl.BlockSpec((1, tn), lambda j: (0, j))],
        out_specs=pl.BlockSpec((8, tn), lambda j: (0, j)),
        out_shape=jax.ShapeDtypeStruct((8, n), f32),
        compiler_params=_params(("parallel",)),
        name="ada_mod",
    )(c8, w_ada, b_ada.reshape(1, n))


def _modulate_kernel(x_ref, sc_ref, sh_ref, u_ref):
    u_ref[...] = (x_ref[...] * (1.0 + sc_ref[...]) + sh_ref[...]).astype(u_ref.dtype)


def _modulate(x, sc, sh):
    s, d = x.shape
    tm = min(256, s)
    row = pl.BlockSpec((1, d), lambda i: (0, 0))
    return pl.pallas_call(
        _modulate_kernel,
        grid=(s // tm,),
        in_specs=[pl.BlockSpec((tm, d), lambda i: (i, 0)), row, row],
        out_specs=pl.BlockSpec((tm, d), lambda i: (i, 0)),
        out_shape=jax.ShapeDtypeStruct((s, d), bf16),
        compiler_params=_params(("parallel",)),
        name="modulate",
    )(x, sc, sh)


W_SLAB = 512


def _mm_kernel(transposed, a_ref, w_ref, o_ref, wb_ref):
    @pl.when(pl.program_id(1) == 0)
    def _():
        for r0 in range(0, wb_ref.shape[0], W_SLAB):
            rs = slice(r0, r0 + W_SLAB)
            if transposed:
                wb_ref[rs, :] = w_ref[0, :, rs].T.astype(bf16)
            else:
                wb_ref[rs, :] = w_ref[rs, :].astype(bf16)

    o_ref[...] = _dot(a_ref[...], wb_ref[...]).astype(o_ref.dtype)


def _matmul(a, w, layer, col0, ncols, out_dtype, tm, tn, name, transposed=False):
    m, k = a.shape
    tm = min(tm, m)
    assert ncols % tn == 0 and k % W_SLAB == 0
    if transposed:
        assert col0 % 8 == 0
        w_spec = pl.BlockSpec((pl.Element(1), pl.Element(tn), pl.Element(k)),
                              lambda j, i: (layer, pl.multiple_of(col0 + j * tn, 8), 0))
    else:
        assert col0 % tn == 0
        w_spec = pl.BlockSpec((None, k, tn), lambda j, i: (layer, 0, j + col0 // tn))
    return pl.pallas_call(
        functools.partial(_mm_kernel, transposed),
        grid=(ncols // tn, m // tm),
        in_specs=[pl.BlockSpec((tm, k), lambda j, i: (i, 0)), w_spec],
        out_specs=pl.BlockSpec((tm, tn), lambda j, i: (i, j)),
        out_shape=jax.ShapeDtypeStruct((m, ncols), out_dtype),
        scratch_shapes=[pltpu.VMEM((k, tn), bf16)],
        compiler_params=_params(("arbitrary", "arbitrary")),
        name=name,
    )(a, w)


def _merge_kernel(oa_ref, ob_ref, wa_ref, wb_ref, ga_ref, gb_ref, o_ref, wa16_ref, wb16_ref):
    @pl.when(pl.program_id(1) == 0)
    def _():
        wa16_ref[...] = wa_ref[...].astype(bf16)
        wb16_ref[...] = wb_ref[...].astype(bf16)

    ya = _dot(oa_ref[...], wa16_ref[...])
    yb = _dot(ob_ref[...], wb16_ref[...])
    o_ref[...] = (jax.nn.sigmoid(ga_ref[...]) * ya
                  + jax.nn.sigmoid(gb_ref[...]) * yb).astype(o_ref.dtype)


def _merge(oa, ob, w_up_a, w_up_b, proj_b, layer):
    s = oa.shape[0]
    tm = min(1024, s)
    tn = 512
    ga0 = (WQB + 2 * WKVB) // tn
    gb0 = ga0 + D_MODEL // tn
    return pl.pallas_call(
        _merge_kernel,
        grid=(D_MODEL // tn, s // tm),
        in_specs=[pl.BlockSpec((tm, WVA), lambda j, i: (i, 0)),
                  pl.BlockSpec((tm, WQB), lambda j, i: (i, 0)),
                  pl.BlockSpec((None, WVA, tn), lambda j, i: (layer, 0, j)),
                  pl.BlockSpec((None, WQB, tn), lambda j, i: (layer, 0, j)),
                  pl.BlockSpec((tm, tn), lambda j, i: (i, j + ga0)),
                  pl.BlockSpec((tm, tn), lambda j, i: (i, j + gb0))],
        out_specs=pl.BlockSpec((tm, tn), lambda j, i: (i, j)),
        out_shape=jax.ShapeDtypeStruct((s, D_MODEL), bf16),
        scratch_shapes=[pltpu.VMEM((WVA, tn), bf16), pltpu.VMEM((WQB, tn), bf16)],
        compiler_params=_params(("arbitrary", "arbitrary")),
        name="gated_merge",
    )(oa, ob, w_up_a, w_up_b, proj_b, proj_b)


def _layernorm(z, g, b):
    mu = jnp.mean(z, axis=-1, keepdims=True)
    zc = z - mu
    var = jnp.mean(zc * zc, axis=-1, keepdims=True)
    return zc * lax.rsqrt(var + LN_EPS) * g + b


def _route(logits):
    tm, n_e = logits.shape
    lane = lax.broadcasted_iota(jnp.int32, logits.shape, 1)
    slot = lax.broadcasted_iota(jnp.int32, (tm, LANES), 1)
    work = logits
    ids, vals = [], []
    for _ in range(TOP_K):
        m = jnp.max(work, axis=-1, keepdims=True)
        idx = jnp.min(jnp.where(work == m, lane, n_e), axis=-1, keepdims=True)
        work = jnp.where(lane == idx, -jnp.inf, work)
        ids.append(idx)
        vals.append(m)
    es = [jnp.exp(v - vals[0]) for v in vals]
    inv = 1.0 / functools.reduce(lambda p, q: p + q, es)
    comb = jnp.zeros(logits.shape, f32)
    id_tile = jnp.zeros((tm, LANES), jnp.int32)
    w_tile = jnp.zeros((tm, LANES), f32)
    for r in range(TOP_K):
        w = es[r] * inv
        comb = jnp.where(lane == ids[r], w, comb)
        id_tile = jnp.where(slot == r, ids[r], id_tile)
        w_tile = jnp.where(slot == r, w, w_tile)
    return comb, id_tile, w_tile


def _ln_route_kernel(x_ref, y_ref, gt_ref, g_ref, b_ref, sc_ref, sh_ref, wr_ref, br_ref,
                     xo_ref, uo_ref, comb_ref, id_ref, w_ref):
    z = DN_ALPHA * x_ref[...] + gt_ref[...] * y_ref[...]
    xn = _layernorm(z, g_ref[...], b_ref[...])
    xo_ref[...] = xn
    u = xn * (1.0 + sc_ref[...]) + sh_ref[...]
    uo_ref[...] = u
    logits = _dot3(u, wr_ref[...]) + br_ref[...]
    comb_ref[...], id_ref[...], w_ref[...] = _route(logits)


def _ln_route(x, y, gt, g, b, sc, sh, w_router, b_router, layer):
    s, d = x.shape
    tm = min(256, s)
    tile = pl.BlockSpec((tm, d), lambda i: (i, 0))
    row = pl.BlockSpec((1, d), lambda i: (0, 0))
    lrow = pl.BlockSpec((None, 1, d), lambda i: (layer, 0, 0))
    small = pl.BlockSpec((tm, LANES), lambda i: (i, 0))
    return pl.pallas_call(
        _ln_route_kernel,
        grid=(s // tm,),
        in_specs=[tile, tile, row, lrow, lrow, row, row,
                  pl.BlockSpec((None, d, N_EXPERTS), lambda i: (layer, 0, 0)),
                  pl.BlockSpec((None, 1, N_EXPERTS), lambda i: (layer, 0, 0))],
        out_specs=[tile, tile, pl.BlockSpec((tm, N_EXPERTS), lambda i: (i, 0)), small, small],
        out_shape=[jax.ShapeDtypeStruct((s, d), f32), jax.ShapeDtypeStruct((s, d), f32),
                   jax.ShapeDtypeStruct((s, N_EXPERTS), f32),
                   jax.ShapeDtypeStruct((s, LANES), jnp.int32),
                   jax.ShapeDtypeStruct((s, LANES), f32)],
        compiler_params=_params(("parallel",)),
        name="residual_ln_route",
    )(x, y, gt, g, b, sc, sh, w_router, b_router)


MOE_TM = 256
DISP_TM = 256
COMB_TM = 128
PAD_BITS = tuple(1 << p for p in reversed(range(int(math.log2(MOE_TM)))))


def _plan(id_tile, s):
    idx = id_tile[:, :TOP_K]
    experts = jnp.arange(N_EXPERTS, dtype=jnp.int32)
    onehot = idx[:, :, None] == experts[None, None, :]
    mask = jnp.any(onehot, axis=1).astype(jnp.int32)
    incl = jnp.cumsum(mask, axis=0)
    counts = incl[-1]
    padded = (counts + MOE_TM - 1) // MOE_TM * MOE_TM
    ends = jnp.cumsum(padded)
    starts = ends - padded
    dest_all = starts[None, :] + incl - mask
    dest = jnp.sum(jnp.where(onehot, dest_all[:, None, :], 0), axis=-1)
    n_tiles = (TOP_K * s) // MOE_TM + N_EXPERTS
    end_tiles = ends // MOE_TM
    n_used = end_tiles[-1]
    tiles = jnp.arange(n_tiles, dtype=jnp.int32)
    tile_expert = jnp.sum((jnp.minimum(tiles, n_used - 1)[:, None] >= end_tiles[None, :])
                          .astype(jnp.int32), axis=1)
    tile_expert = jnp.minimum(tile_expert, N_EXPERTS - 1)
    return (dest.astype(jnp.int32), tile_expert.astype(jnp.int32),
            n_used.reshape(1).astype(jnp.int32), (starts + counts).astype(jnp.int32),
            (padded - counts).astype(jnp.int32), n_tiles)


def _dispatch_kernel(pad_row_ref, pad_len_ref, used_ref, dest_ref, u_ref, us_ref, zero_ref, sem, zsem):
    i = pl.program_id(0)
    tm = u_ref.shape[0]

    for r in range(tm):
        for k in range(TOP_K):
            row = dest_ref[0, r * TOP_K + k]
            pltpu.make_async_copy(u_ref.at[pl.ds(r, 1)], us_ref.at[pl.ds(row, 1)], sem).start()

    @pl.when(i == pl.num_programs(0) - 1)
    def _():
        zero_ref[...] = jnp.zeros_like(zero_ref)

        def fill(wait, e, carry):
            row = pad_row_ref[e]
            n = pad_len_ref[e]
            for bit in reversed(PAD_BITS):
                size = bit if bit >= 8 else 1
                for part in range(bit // size):
                    @pl.when((n & bit) != 0)
                    def _():
                        at = row + part * size
                        if size >= 8:
                            at = pl.multiple_of(at, 8)
                        cp = pltpu.make_async_copy(zero_ref.at[pl.ds(0, size)],
                                                   us_ref.at[pl.ds(at, size)], zsem)
                        if wait:
                            cp.wait()
                        else:
                            cp.start()
                row = row + (n & bit)
            return carry

        def tail(wait, t, carry):
            tile = used_ref[0] + t

            @pl.when(tile < us_ref.shape[0] // MOE_TM)
            def _():
                for part in range(MOE_TM // zero_ref.shape[0]):
                    at = pl.multiple_of(tile * MOE_TM + part * zero_ref.shape[0], 8)
                    cp = pltpu.make_async_copy(
                        zero_ref, us_ref.at[pl.ds(at, zero_ref.shape[0])], zsem)
                    if wait:
                        cp.wait()
                    else:
                        cp.start()
            return carry

        for wait in (False, True):
            lax.fori_loop(0, N_EXPERTS, functools.partial(fill, wait), 0)
            lax.fori_loop(0, N_EXPERTS, functools.partial(tail, wait), 0)

    for _ in range(TOP_K):
        pltpu.make_async_copy(u_ref, us_ref.at[pl.ds(0, tm)], sem).wait()


def _dispatch(u, dest, pad_row, pad_len, n_used, n_tiles):
    s, d = u.shape
    tm = min(DISP_TM, s)
    dest3 = dest.reshape(s // tm, 1, tm * TOP_K)
    return pl.pallas_call(
        _dispatch_kernel,
        grid_spec=pltpu.PrefetchScalarGridSpec(
            num_scalar_prefetch=3,
            grid=(s // tm,),
            in_specs=[pl.BlockSpec((None, 1, tm * TOP_K), lambda i, *_: (i, 0, 0),
                                   memory_space=pltpu.SMEM),
                      pl.BlockSpec((tm, d), lambda i, *_: (i, 0))],
            out_specs=pl.BlockSpec(memory_space=pl.ANY),
            scratch_shapes=[pltpu.VMEM((MOE_TM // 2, d), f32),
                            pltpu.SemaphoreType.DMA(()), pltpu.SemaphoreType.DMA(())]),
        out_shape=jax.ShapeDtypeStruct((n_tiles * MOE_TM, d), f32),
        compiler_params=_params(("arbitrary",)),
        name="moe_dispatch",
    )(pad_row, pad_len, n_used, dest3, u)


def _experts_kernel(te_ref, nu_ref, x_ref, wgu_ref, bgu_ref, wd_ref, y_ref, wgu16_ref, wd16_ref):
    i = pl.program_id(0)
    new_expert = jnp.logical_or(i == 0, te_ref[i] != te_ref[jnp.maximum(i - 1, 0)])

    @pl.when(new_expert)
    def _():
        for r0 in range(0, wgu_ref.shape[0], W_SLAB):
            rs = slice(r0, r0 + W_SLAB)
            wgu16_ref[rs, :] = wgu_ref[rs, :].astype(bf16)
        wd16_ref[...] = wd_ref[...].astype(bf16)

    @pl.when(i < nu_ref[0])
    def _():
        gu = _dot(x_ref[...].astype(bf16), wgu16_ref[...]) + bgu_ref[...]
        gate = jnp.minimum(gu[:, :D_EXPERT], SWIGLU_LIMIT)
        up = jnp.clip(gu[:, D_EXPERT:], -SWIGLU_LIMIT, SWIGLU_LIMIT)
        h = (up + 1.0) * gate * jax.nn.sigmoid(SWIGLU_ALPHA * gate)
        y_ref[...] = _dot(h.astype(bf16), wd16_ref[...])

    @pl.when(i >= nu_ref[0])
    def _():
        y_ref[...] = jnp.zeros_like(y_ref)


def _experts(us, tile_expert, n_used, w_gate_up, b_gate_up, w_down, layer):
    rows, d = us.shape
    return pl.pallas_call(
        _experts_kernel,
        grid_spec=pltpu.PrefetchScalarGridSpec(
            num_scalar_prefetch=2,
            grid=(rows // MOE_TM,),
            in_specs=[pl.BlockSpec((MOE_TM, d), lambda i, te, nu: (jnp.minimum(i, nu[0] - 1), 0)),
                      pl.BlockSpec((None, None, d, 2 * D_EXPERT),
                                   lambda i, te, nu: (layer, te[i], 0, 0)),
                      pl.BlockSpec((None, None, 1, 2 * D_EXPERT),
                                   lambda i, te, nu: (layer, te[i], 0, 0)),
                      pl.BlockSpec((None, None, D_EXPERT, d),
                                   lambda i, te, nu: (layer, te[i], 0, 0))],
            out_specs=pl.BlockSpec((MOE_TM, d), lambda i, te, nu: (i, 0)),
            scratch_shapes=[pltpu.VMEM((d, 2 * D_EXPERT), bf16),
                            pltpu.VMEM((D_EXPERT, d), bf16)]),
        out_shape=jax.ShapeDtypeStruct((rows, d), f32),
        compiler_params=_params(("arbitrary",)),
        name="moe_experts",
    )(tile_expert, n_used, us, w_gate_up, b_gate_up, w_down)


def _combine_ln_kernel(dest_ref, dnext_ref, x_ref, w_ref, comb_ref, bd_ref, gt_ref, g_ref, b_ref,
                       sc_ref, sh_ref, ys_ref, xo_ref, uo_ref, buf_ref, sem):
    i = pl.program_id(0)
    n = pl.num_programs(0)
    tm = x_ref.shape[0]
    slot = i % 2

    def gather(d_ref, to):
        for r in range(tm):
            for k in range(TOP_K):
                row = d_ref[0, r * TOP_K + k]
                pltpu.make_async_copy(ys_ref.at[pl.ds(row, 1)],
                                      buf_ref.at[to, k, pl.ds(r, 1)], sem.at[to]).start()

    def wait(at):
        for k in range(TOP_K):
            pltpu.make_async_copy(ys_ref.at[pl.ds(0, tm)], buf_ref.at[at, k], sem.at[at]).wait()

    @pl.when(i == 0)
    def _():
        gather(dest_ref, 0)

    wait(slot)
    gather(dnext_ref, 1 - slot)

    w_tile = w_ref[...]
    lane = lax.broadcasted_iota(jnp.int32, w_tile.shape, 1)
    y = _dot3(comb_ref[...], bd_ref[...])
    for k in range(TOP_K):
        w_k = jnp.sum(jnp.where(lane == k, w_tile, 0.0), axis=1, keepdims=True)
        y = y + w_k * buf_ref[slot, k]
    z = DN_ALPHA * x_ref[...] + gt_ref[...] * y
    xn = _layernorm(z, g_ref[...], b_ref[...])
    xo_ref[...] = xn
    uo_ref[...] = (xn * (1.0 + sc_ref[...]) + sh_ref[...]).astype(uo_ref.dtype)

    @pl.when(i == n - 1)
    def _():
        wait(1 - slot)


def _combine_ln(x, ys, dest, w_tile, comb, b_down, gt, g, b, sc, sh, layer):
    s, d = x.shape
    tm = min(COMB_TM, s)
    steps = s // tm
    dest3 = dest.reshape(steps, 1, tm * TOP_K)
    tile = pl.BlockSpec((tm, d), lambda i: (i, 0))
    row = pl.BlockSpec((1, d), lambda i: (0, 0))
    lrow = pl.BlockSpec((None, 1, d), lambda i: (layer, 0, 0))
    return pl.pallas_call(
        _combine_ln_kernel,
        grid=(steps,),
        in_specs=[pl.BlockSpec((None, 1, tm * TOP_K), lambda i: (i, 0, 0), memory_space=pltpu.SMEM),
                  pl.BlockSpec((None, 1, tm * TOP_K),
                               lambda i: (jnp.minimum(i + 1, steps - 1), 0, 0),
                               memory_space=pltpu.SMEM),
                  tile,
                  pl.BlockSpec((tm, LANES), lambda i: (i, 0)),
                  pl.BlockSpec((tm, N_EXPERTS), lambda i: (i, 0)),
                  pl.BlockSpec((None, N_EXPERTS, d), lambda i: (layer, 0, 0)),
                  row, lrow, lrow, row, row,
                  pl.BlockSpec(memory_space=pl.ANY)],
        out_specs=[tile, tile],
        out_shape=[jax.ShapeDtypeStruct((s, d), f32), jax.ShapeDtypeStruct((s, d), bf16)],
        scratch_shapes=[pltpu.VMEM((2, TOP_K, tm, d), f32), pltpu.SemaphoreType.DMA((2,))],
        compiler_params=_params(("arbitrary",)),
        name="moe_combine_ln",
    )(dest3, dest3, x, w_tile, comb, b_down, gt, g, b, sc, sh, ys)


def _silu(x):
    return x * jax.nn.sigmoid(x)


def _l2norm(x):
    return x * lax.rsqrt(jnp.sum(x * x, axis=-1, keepdims=True) + NORM_EPS)


def _ba_kernel(u_ref, w_ref, alog_ref, dt_ref, o_ref):
    tm = u_ref.shape[0]
    ba = _dot_nt(u_ref[...], w_ref[...].astype(bf16))
    o_ref[:, :LANES] = jax.nn.sigmoid(ba)
    xa = ba + dt_ref[...]
    softplus = jnp.maximum(xa, 0.0) + jnp.log(1.0 + jnp.exp(-jnp.abs(xa)))
    g = -jnp.exp(alog_ref[...]) * softplus
    row_i = lax.broadcasted_iota(jnp.int32, (CHUNK, CHUNK), 0)
    col_i = lax.broadcasted_iota(jnp.int32, (CHUNK, CHUNK), 1)
    tril = jnp.where(row_i >= col_i, 1.0, 0.0).astype(bf16)
    for c in range(tm // CHUNK):
        r = slice(c * CHUNK, (c + 1) * CHUNK)
        g1 = g[r].astype(bf16)
        r1 = g[r] - g1.astype(f32)
        g2 = r1.astype(bf16)
        g3 = (r1 - g2.astype(f32)).astype(bf16)
        o_ref[r, LANES:] = _dot(tril, g1) + _dot(tril, g2) + _dot(tril, g3)


def _ba(u, w_in_t, alog, dtb, layer):
    s, d = u.shape
    tm = min(1024, s)
    lrow = pl.BlockSpec((None, 1, LANES), lambda i: (layer, 0, 0))
    return pl.pallas_call(
        _ba_kernel,
        grid=(s // tm,),
        in_specs=[pl.BlockSpec((tm, d), lambda i: (i, 0)),
                  pl.BlockSpec((None, LANES, d), lambda i: (layer, OFF_BA // LANES, 0)),
                  lrow, lrow],
        out_specs=pl.BlockSpec((tm, 2 * LANES), lambda i: (i, 0)),
        out_shape=jax.ShapeDtypeStruct((s, 2 * LANES), f32),
        compiler_params=_params(("parallel",)),
        name="beta_decay",
    )(u, w_in_t, alog, dtb)


def _delta_kernel(heads,
                  q_ref, k_ref, v_ref, z_ref, qh_ref, kh_ref, vh_ref, beta_ref, cum_ref,
                  cq_ref, ck_ref, cv_ref, na_ref,
                  o_ref, state_ref, xs_ref, cumt_ref):
    hp = pl.program_id(0)
    i = pl.program_id(1)

    @pl.when(i == 0)
    def _():
        state_ref[...] = jnp.zeros_like(state_ref)

    first = i == 0
    row_i = lax.broadcasted_iota(jnp.int32, (CHUNK, CHUNK), 0)
    col_i = lax.broadcasted_iota(jnp.int32, (CHUNK, CHUNK), 1)
    causal = row_i >= col_i
    strict = row_i > col_i
    eye = jnp.where(row_i == col_i, 1.0, 0.0)
    merge_masks = []
    for shift in range(int(math.log2(CHUNK))):
        rb = row_i >> shift
        merge_masks.append(jnp.logical_and((rb & 1) == 1, (col_i >> shift) == rb - 1))

    def conv_silu(x_ref, halo_ref, cw_ref):
        xs_ref[0:8, :] = jnp.where(first, 0.0, halo_ref[...])
        xs_ref[8:8 + CHUNK, :] = x_ref[...]
        acc = cw_ref[CONV_K - 1:CONV_K, :] * xs_ref[8:8 + CHUNK, :]
        for back in range(1, CONV_K):
            tap = CONV_K - 1 - back
            acc = acc + cw_ref[tap:tap + 1, :] * xs_ref[8 - back:8 - back + CHUNK, :]
        return _silu(acc)

    q_all = conv_silu(q_ref, qh_ref, cq_ref)
    k_all = conv_silu(k_ref, kh_ref, ck_ref)
    v_all = conv_silu(v_ref, vh_ref, cv_ref)
    beta_t = beta_ref[...]
    cum_tile = cum_ref[...]
    cumt_ref[...] = cum_tile.T

    hs = range(heads)
    sls = [slice(j * LANES, (j + 1) * LANES) for j in hs]
    q, k, v, beta_b, cum, decay = [], [], [], [], [], []
    for j in hs:
        head = hp * heads + j
        q.append(_l2norm(q_all[:, sls[j]]) * (DKA ** -0.5))
        k.append(_l2norm(k_all[:, sls[j]]))
        v.append(v_all[:, sls[j]])
        beta_b.append(jnp.broadcast_to(
            jnp.sum(jnp.where(col_i == head, beta_t, 0.0), axis=1, keepdims=True), (CHUNK, CHUNK)))
        cum.append(jnp.broadcast_to(
            jnp.sum(jnp.where(col_i == head + HA, cum_tile, 0.0), axis=1, keepdims=True),
            (CHUNK, CHUNK)))
        cum_t = jnp.broadcast_to(cumt_ref[pl.ds(head + HA, 1), :], (CHUNK, CHUNK))
        decay.append(jnp.exp(jnp.where(causal, cum[j] - cum_t, NEG_BIG)))
    k16 = [k[j].astype(bf16) for j in hs]
    kb = [k[j] * beta_b[j] for j in hs]
    lmat = [jnp.where(strict, _dot_nt(kb[j].astype(bf16), k16[j]) * decay[j], 0.0) for j in hs]
    intra = [jnp.where(causal, _dot_nt(q[j].astype(bf16), k16[j]) * decay[j], 0.0).astype(bf16)
             for j in hs]
    tn = [jnp.where(merge_masks[0], -lmat[j], 0.0) for j in hs]
    for mask in merge_masks[1:]:
        t_inv = [(eye + tn[j]).astype(bf16) for j in hs]
        half = [_dot(t_inv[j], jnp.where(mask, lmat[j], 0.0).astype(bf16)).astype(bf16) for j in hs]
        tn = [tn[j] - _dot(half[j], t_inv[j]) for j in hs]
    e_g = [jnp.exp(cum[j]) for j in hs]
    g_last = [cum[j][CHUNK - 1:CHUNK, :] for j in hs]
    rhs = [jnp.concatenate([v[j] * beta_b[j], kb[j] * e_g[j]], axis=1) for j in hs]
    uw = [rhs[j] + _dot(tn[j].astype(bf16), rhs[j].astype(bf16)) for j in hs]
    k_dec_t = [(k[j] * jnp.exp(g_last[j] - cum[j])).T.astype(bf16) for j in hs]
    state = [state_ref[j] for j in hs]
    ws = [_dot(jnp.concatenate([uw[j][:, DVA:], q[j] * e_g[j]], axis=0).astype(bf16),
               state[j].astype(bf16)) for j in hs]
    v16 = [(uw[j][:, :DVA] - ws[j][:CHUNK]).astype(bf16) for j in hs]
    o_c = [ws[j][CHUNK:] + _dot(intra[j], v16[j]) for j in hs]
    for j in hs:
        state_ref[j] = state[j] * jnp.exp(g_last[j]) + _dot(k_dec_t[j], v16[j])
    for j in hs:
        o_n = o_c[j] * lax.rsqrt(jnp.mean(o_c[j] * o_c[j], axis=-1, keepdims=True) + NORM_EPS)
        o_ref[:, sls[j]] = (o_n * na_ref[...] * _silu(z_ref[:, sls[j]])).astype(o_ref.dtype)


def _delta(proj_a, bg, conv_w, norm_a, layer):
    s = proj_a.shape[0]
    heads = 8
    hw = heads * LANES
    halo_blocks = CHUNK // 8

    def tile(col0):
        return pl.BlockSpec((CHUNK, hw), lambda hp, i: (i, hp + col0 // hw))

    def halo(col0):
        return pl.BlockSpec(
            (8, hw), lambda hp, i: (jnp.maximum(i * halo_blocks - 1, 0), hp + col0 // hw))

    def cw(col0):
        return pl.BlockSpec((None, CONV_K, hw), lambda hp, i: (layer, 0, hp + col0 // hw))

    return pl.pallas_call(
        functools.partial(_delta_kernel, heads),
        grid=(WA // hw, s // CHUNK),
        in_specs=[tile(0), tile(WA), tile(2 * WA), tile(2 * WA + WVA),
                  halo(0), halo(WA), halo(2 * WA),
                  pl.BlockSpec((CHUNK, LANES), lambda hp, i: (i, 0)),
                  pl.BlockSpec((CHUNK, LANES), lambda hp, i: (i, 1)),
                  cw(0), cw(WA), cw(2 * WA),
                  pl.BlockSpec((None, 1, LANES), lambda hp, i: (layer, 0, 0))],
        out_specs=pl.BlockSpec((CHUNK, hw), lambda hp, i: (i, hp)),
        out_shape=jax.ShapeDtypeStruct((s, WVA), bf16),
        scratch_shapes=[pltpu.VMEM((heads, DKA, DVA), f32),
                        pltpu.VMEM((CHUNK + 8, hw), f32),
                        pltpu.VMEM((LANES, CHUNK), f32)],
        compiler_params=_params(("parallel", "arbitrary")),
        name="gated_delta",
    )(proj_a, proj_a, proj_a, proj_a, proj_a, proj_a, proj_a, bg, bg,
      conv_w, conv_w, conv_w, norm_a)


def _attn_kernel(layer, q_ref, kp_ref, kc_ref, vp_ref, vc_ref, bm_ref, sink_ref, o_ref):
    n = pl.program_id(0)
    lane = lax.broadcasted_iota(jnp.int32, (2 * BLK, LANES), 1)
    low = lane < DHB
    key_j = lax.broadcasted_iota(jnp.int32, (BLK, 2 * BLK), 1)
    no_prev = jnp.logical_and(n == 0, key_j < BLK)
    scale = DHB ** -0.5
    groups_per_tile = LANES // DHB
    pairs_per_group = HB // HB_KV // 2
    k_cat, v_cat = [], []
    for t in range(WKVB // LANES):
        sl = slice(t * LANES, (t + 1) * LANES)
        k_t = jnp.concatenate([kp_ref[:, sl], kc_ref[:, sl]], axis=0)
        v_t = jnp.concatenate([vp_ref[:, sl], vc_ref[:, sl]], axis=0)
        for gg in range(groups_per_tile):
            mine = low if gg == 0 else jnp.logical_not(low)
            k_m = jnp.where(mine, k_t, 0.0)
            v_m = jnp.where(mine, v_t, 0.0)
            k_2 = k_m + pltpu.roll(k_m, DHB, axis=1)
            v_2 = v_m + pltpu.roll(v_m, DHB, axis=1)
            k_cat.append(jnp.concatenate([jnp.where(low, k_2, 0.0),
                                          jnp.where(low, 0.0, k_2)], axis=0).astype(bf16))
            v_cat.append(jnp.concatenate([jnp.where(low, v_2, 0.0),
                                          jnp.where(low, 0.0, v_2)], axis=0).astype(bf16))
    pairs = range(HB // 2)
    qls = [slice(p * LANES, (p + 1) * LANES) for p in pairs]
    scores = [_dot_nt((q_ref[:, qls[p]] * scale).astype(bf16), k_cat[p // pairs_per_group])
              for p in pairs]
    probs = []
    for p in pairs:
        halves = []
        for hh in range(2):
            head = 2 * p + hh
            sh = scores[p][:, hh * 2 * BLK:(hh + 1) * 2 * BLK] + bm_ref[head]
            sh = jnp.where(no_prev, NEG_BIG, sh)
            sink = sink_ref[layer, head]
            mx = jnp.maximum(jnp.max(sh, axis=-1, keepdims=True), sink)
            e = jnp.exp(sh - mx)
            den = jnp.sum(e, axis=-1, keepdims=True) + jnp.exp(sink - mx)
            halves.append((e * (1.0 / den)).astype(bf16))
        probs.append(jnp.concatenate(halves, axis=1))
    for p in pairs:
        o_ref[:, qls[p]] = _dot(probs[p], v_cat[p // pairs_per_group]).astype(o_ref.dtype)


def _attn(proj_b, bias_mask, sinks, layer):
    s = proj_b.shape[0]
    kb0 = WQB // WKVB
    return pl.pallas_call(
        functools.partial(_attn_kernel, layer),
        grid=(s // BLK,),
        in_specs=[pl.BlockSpec((BLK, WQB), lambda n: (n, 0)),
                  pl.BlockSpec((BLK, WKVB), lambda n: (jnp.maximum(n - 1, 0), kb0)),
                  pl.BlockSpec((BLK, WKVB), lambda n: (n, kb0)),
                  pl.BlockSpec((BLK, WKVB), lambda n: (jnp.maximum(n - 1, 0), kb0 + 1)),
                  pl.BlockSpec((BLK, WKVB), lambda n: (n, kb0 + 1)),
                  pl.BlockSpec((HB, BLK, 2 * BLK), lambda n: (0, 0, 0)),
                  pl.BlockSpec(memory_space=pltpu.SMEM)],
        out_specs=pl.BlockSpec((BLK, WQB), lambda n: (n, 0)),
        out_shape=jax.ShapeDtypeStruct((s, WQB), bf16),
        compiler_params=_params(("parallel",)),
        name="swa_attention",
    )(proj_b, proj_b, proj_b, proj_b, proj_b, bias_mask, sinks)


def _t5_bucket(d):
    n_exact = N_BUCKETS // 2
    df = jnp.maximum(d, 1).astype(f32)
    large = n_exact + (jnp.log(df / n_exact) / math.log(MAX_DIST / n_exact)
                       * (N_BUCKETS - n_exact)).astype(jnp.int32)
    large = jnp.minimum(large, N_BUCKETS - 1)
    return jnp.where(d < n_exact, d, large)


def _bias_kernel(rbt_ref, bucket_ref, inside_ref, o_ref):
    n = bucket_ref.shape[1]
    rows = lax.broadcasted_iota(jnp.int32, (N_BUCKETS, n), 0)
    onehot = jnp.where(rows == bucket_ref[...], 1.0, 0.0).astype(bf16)
    r = rbt_ref[...]
    r1 = r.astype(bf16)
    rem = r - r1.astype(f32)
    r2 = rem.astype(bf16)
    r3 = (rem - r2.astype(f32)).astype(bf16)
    table = _dot(r1, onehot) + _dot(r2, onehot) + _dot(r3, onehot)
    o_ref[...] = jnp.where(inside_ref[...] != 0, table, NEG_BIG)


def _band_bias(rel_bias):
    i = jnp.arange(BLK)[:, None]
    j = jnp.arange(2 * BLK)[None, :]
    d = i + BLK - j
    inside = ((d >= 0) & (d < WINDOW)).astype(jnp.int32).reshape(1, -1)
    bucket = _t5_bucket(jnp.clip(d, 0, MAX_DIST - 1)).astype(jnp.int32).reshape(1, -1)
    n = bucket.shape[1]
    tn = 4096
    flat = pl.pallas_call(
        _bias_kernel,
        grid=(n // tn,),
        in_specs=[pl.BlockSpec((HB, N_BUCKETS), lambda c: (0, 0)),
                  pl.BlockSpec((1, tn), lambda c: (0, c)),
                  pl.BlockSpec((1, tn), lambda c: (0, c))],
        out_specs=pl.BlockSpec((HB, tn), lambda c: (0, c)),
        out_shape=jax.ShapeDtypeStruct((HB, n), f32),
        compiler_params=_params(("parallel",)),
        name="band_bias",
    )(rel_bias.astype(f32).T, bucket, inside)
    return flat.reshape(HB, BLK, 2 * BLK)


def kernel(x, c, w_ada, b_ada, ada_table, rel_bias, w_in, conv_w, a_log, dt_bias, norm_a, sinks,
           w_up_a, w_up_b, w_o, ln1_g, ln1_b, w_router, b_router, w_gate_up, b_gate_up, w_down,
           b_down, ln2_g, ln2_b):
    bsz, s, d = x.shape
    assert bsz == 1 and d == D_MODEL and s % BLK == 0
    depth = w_in.shape[0]
    xs = x.reshape(s, d)

    mod_base = _ada(jnp.broadcast_to(c, (8, d)), w_ada, b_ada)[0]
    mod = mod_base.reshape(1, N_MOD, d) + ada_table

    b_gu = b_gate_up.reshape(depth, N_EXPERTS, 1, 2 * D_EXPERT)
    b_rt = b_router.reshape(depth, 1, N_EXPERTS)
    pad_h = ((0, 0), (HA, LANES - 2 * HA))
    alog = jnp.pad(a_log, pad_h).reshape(depth, 1, LANES)
    dtb = jnp.pad(dt_bias, pad_h).reshape(depth, 1, LANES)
    na = norm_a.reshape(depth, 1, DVA)
    g1 = ln1_g.reshape(depth, 1, d)
    b1 = ln1_b.reshape(depth, 1, d)
    g2 = ln2_g.reshape(depth, 1, d)
    b2 = ln2_b.reshape(depth, 1, d)
    bias_mask = _band_bias(rel_bias)
    w_in_t = jnp.swapaxes(w_in, 1, 2)

    def mrow(layer, idx):
        return mod[layer, idx].reshape(1, d)

    u = _modulate(xs, mrow(0, 1), mrow(0, 0))
    for layer in range(depth):
        proj_a = _matmul(u, w_in_t, layer, 0, W_PROJ_A, f32, 1024, 512, "in_proj_a", True)
        proj_b = _matmul(u, w_in_t, layer, OFF_B, W_PROJ_B, f32, 1024, 512, "in_proj_b", True)
        bg = _ba(u, w_in_t, alog, dtb, layer)
        o_a = _delta(proj_a, bg, conv_w, na, layer)
        o_b = _attn(proj_b, bias_mask, sinks, layer)
        merged = _merge(o_a, o_b, w_up_a, w_up_b, proj_b, layer)
        y = _matmul(merged, w_o, layer, 0, d, f32, 1024, 512, "out_proj")
        xs, u_moe, comb, id_tile, w_tile = _ln_route(
            xs, y, mrow(layer, 2), g1, b1, mrow(layer, 4), mrow(layer, 3), w_router, b_rt, layer)
        dest, tile_expert, n_used, pad_row, pad_len, n_tiles = _plan(id_tile, s)
        us = _dispatch(u_moe, dest, pad_row, pad_len, n_used, n_tiles)
        ys = _experts(us, tile_expert, n_used, w_gate_up, b_gu, w_down, layer)
        nxt = min(layer + 1, depth - 1)
        xs, u = _combine_ln(xs, ys, dest, w_tile, comb, b_down, mrow(layer, 5), g2, b2,
                            mrow(nxt, 1), mrow(nxt, 0), layer)
    return xs.reshape(bsz, s, d)
```

```python
import functools
import math

import jax
import jax.numpy as jnp
import numpy as np
from jax import lax
from jax.experimental import pallas as pl
from jax.experimental.pallas import tpu as pltpu

D_MODEL = 4096
DEPTH = 4
HA = 16
DKA = 128
DVA = 128
CONV_K = 4
HB = 32
HB_KV = 8
DHB = 64
WINDOW = 128
BLK = 128
N_BUCKETS = 32
MAX_DIST = WINDOW
N_EXPERTS = 32
TOP_K = 4
D_EXPERT = 256
SWIGLU_LIMIT = 7.0
SWIGLU_ALPHA = 1.702
DN_ALPHA = (2 * DEPTH) ** 0.25
LN_EPS = 1e-5
NORM_EPS = 1e-6
N_MOD = 6
WA = HA * DKA
WVA = HA * DVA
WQB = HB * DHB
WKVB = HB_KV * DHB
OFF_BA = 2 * WA + 2 * WVA
OFF_B = OFF_BA + 2 * HA
W_PROJ_A = OFF_BA
W_PROJ_B = WQB + 2 * WKVB + 2 * D_MODEL

LANES = 128
CHUNK = 128
VMEM_LIMIT = 56 * 1024 * 1024
NEG_BIG = -1e30

bf16 = jnp.bfloat16
f32 = jnp.float32


def _dot(a, b):
    return jnp.dot(a, b, preferred_element_type=f32)


def _dot_nt(a, b):
    return lax.dot_general(a, b, (((1,), (1,)), ((), ())), preferred_element_type=f32)


def _split2(x):
    hi = x.astype(bf16)
    lo = (x - hi.astype(f32)).astype(bf16)
    return hi, lo


def _dot3(a, b):
    ah, al = _split2(a)
    bh, bl = _split2(b)
    return _dot(ah, bh) + _dot(ah, bl) + _dot(al, bh)


def _params(sem):
    return pltpu.CompilerParams(dimension_semantics=sem, vmem_limit_bytes=VMEM_LIMIT)


def _ada_kernel(c_ref, w_ref, b_ref, o_ref):
    c = c_ref[...]
    a = c * jax.nn.sigmoid(c)
    o_ref[...] = _dot3(a, w_ref[...]) + b_ref[...]


def _ada(c8, w_ada, b_ada):
    n = w_ada.shape[1]
    tn = 512
    return pl.pallas_call(
        _ada_kernel,
        grid=(n // tn,),
        in_specs=[pl.BlockSpec((8, D_MODEL), lambda j: (0, 0)),
                  pl.BlockSpec((D_MODEL, tn), lambda j: (0, j)),
                  pl.BlockSpec((1, tn), lambda j: (0, j))],
        out_specs=pl.BlockSpec((8, tn), lambda j: (0, j)),
        out_shape=jax.ShapeDtypeStruct((8, n), f32),
        compiler_params=_params(("parallel",)),
        name="ada_mod",
    )(c8, w_ada, b_ada.reshape(1, n))


def _modulate_kernel(x_ref, sc_ref, sh_ref, u_ref):
    u_ref[...] = (x_ref[...] * (1.0 + sc_ref[...]) + sh_ref[...]).astype(u_ref.dtype)


def _modulate(x, sc, sh):
    s, d = x.shape
    tm = min(256, s)
    row = pl.BlockSpec((1, d), lambda i: (0, 0))
    return pl.pallas_call(
        _modulate_kernel,
        grid=(s // tm,),
        in_specs=[pl.BlockSpec((tm, d), lambda i: (i, 0)), row, row],
        out_specs=pl.BlockSpec((tm, d), lambda i: (i, 0)),
        out_shape=jax.ShapeDtypeStruct((s, d), bf16),
        compiler_params=_params(("parallel",)),
        name="modulate",
    )(x, sc, sh)


W_SLAB = 512


def _mm_kernel(transposed, a_ref, w_ref, o_ref, wb_ref):
    first = pl.program_id(1) == 0

    @pl.when(first)
    def _():
        if transposed:
            o_ref[...] = _dot_nt(a_ref[...], w_ref[0].astype(bf16)).astype(o_ref.dtype)
            for r0 in range(0, wb_ref.shape[0], W_SLAB):
                rs = slice(r0, r0 + W_SLAB)
                wb_ref[rs, :] = w_ref[0, :, rs].T.astype(bf16)
        else:
            w16 = w_ref[...].astype(bf16)
            o_ref[...] = _dot(a_ref[...], w16).astype(o_ref.dtype)
            wb_ref[...] = w16

    @pl.when(jnp.logical_not(first))
    def _():
        o_ref[...] = _dot(a_ref[...], wb_ref[...]).astype(o_ref.dtype)


def _matmul(a, w, layer, col0, ncols, out_dtype, tm, tn, name, transposed=False):
    m, k = a.shape
    tm = min(tm, m)
    assert ncols % tn == 0 and k % W_SLAB == 0
    if transposed:
        assert col0 % 8 == 0
        w_spec = pl.BlockSpec((pl.Element(1), pl.Element(tn), pl.Element(k)),
                              lambda j, i: (layer, pl.multiple_of(col0 + j * tn, 8), 0))
    else:
        assert col0 % tn == 0
        w_spec = pl.BlockSpec((None, k, tn), lambda j, i: (layer, 0, j + col0 // tn))
    return pl.pallas_call(
        functools.partial(_mm_kernel, transposed),
        grid=(ncols // tn, m // tm),
        in_specs=[pl.BlockSpec((tm, k), lambda j, i: (i, 0)), w_spec],
        out_specs=pl.BlockSpec((tm, tn), lambda j, i: (i, j)),
        out_shape=jax.ShapeDtypeStruct((m, ncols), out_dtype),
        scratch_shapes=[pltpu.VMEM((k, tn), bf16)],
        compiler_params=_params(("arbitrary", "arbitrary")),
        name=name,
    )(a, w)


def _merge_kernel(oa_ref, ob_ref, wa_ref, wb_ref, ga_ref, gb_ref, o_ref, wa16_ref, wb16_ref):
    @pl.when(pl.program_id(1) == 0)
    def _():
        wa16_ref[...] = wa_ref[...].astype(bf16)
        wb16_ref[...] = wb_ref[...].astype(bf16)

    ya = _dot(oa_ref[...], wa16_ref[...])
    yb = _dot(ob_ref[...], wb16_ref[...])
    o_ref[...] = (jax.nn.sigmoid(ga_ref[...]) * ya
                  + jax.nn.sigmoid(gb_ref[...]) * yb).astype(o_ref.dtype)


def _merge(oa, ob, w_up_a, w_up_b, proj_b, layer):
    s = oa.shape[0]
    tm = min(1024, s)
    tn = 512
    ga0 = (WQB + 2 * WKVB) // tn
    gb0 = ga0 + D_MODEL // tn
    return pl.pallas_call(
        _merge_kernel,
        grid=(D_MODEL // tn, s // tm),
        in_specs=[pl.BlockSpec((tm, WVA), lambda j, i: (i, 0)),
                  pl.BlockSpec((tm, WQB), lambda j, i: (i, 0)),
                  pl.BlockSpec((None, WVA, tn), lambda j, i: (layer, 0, j)),
                  pl.BlockSpec((None, WQB, tn), lambda j, i: (layer, 0, j)),
                  pl.BlockSpec((tm, tn), lambda j, i: (i, j + ga0)),
                  pl.BlockSpec((tm, tn), lambda j, i: (i, j + gb0))],
        out_specs=pl.BlockSpec((tm, tn), lambda j, i: (i, j)),
        out_shape=jax.ShapeDtypeStruct((s, D_MODEL), bf16),
        scratch_shapes=[pltpu.VMEM((WVA, tn), bf16), pltpu.VMEM((WQB, tn), bf16)],
        compiler_params=_params(("arbitrary", "arbitrary")),
        name="gated_merge",
    )(oa, ob, w_up_a, w_up_b, proj_b, proj_b)


def _layernorm(z, g, b):
    mu = jnp.mean(z, axis=-1, keepdims=True)
    zc = z - mu
    var = jnp.mean(zc * zc, axis=-1, keepdims=True)
    return zc * lax.rsqrt(var + LN_EPS) * g + b


def _route(logits):
    tm, n_e = logits.shape
    lane = lax.broadcasted_iota(jnp.int32, logits.shape, 1)
    slot = lax.broadcasted_iota(jnp.int32, (tm, LANES), 1)
    work = logits
    ids, vals = [], []
    for _ in range(TOP_K):
        m = jnp.max(work, axis=-1, keepdims=True)
        idx = jnp.min(jnp.where(work == m, lane, n_e), axis=-1, keepdims=True)
        work = jnp.where(lane == idx, -jnp.inf, work)
        ids.append(idx)
        vals.append(m)
    es = [jnp.exp(v - vals[0]) for v in vals]
    inv = 1.0 / functools.reduce(lambda p, q: p + q, es)
    comb = jnp.zeros(logits.shape, f32)
    id_tile = jnp.zeros((tm, LANES), jnp.int32)
    w_tile = jnp.zeros((tm, LANES), f32)
    for r in range(TOP_K):
        w = es[r] * inv
        comb = jnp.where(lane == ids[r], w, comb)
        id_tile = jnp.where(slot == r, ids[r], id_tile)
        w_tile = jnp.where(slot == r, w, w_tile)
    return comb, id_tile, w_tile


def _pack_halves(x):
    n = x.shape[1] // 2
    lo = pltpu.bitcast(x[:, :n].astype(bf16).astype(f32), jnp.uint32)
    hi = pltpu.bitcast(x[:, n:].astype(bf16).astype(f32), jnp.uint32)
    return (lo >> 16) | (hi & jnp.uint32(0xFFFF0000))


def _unpack_halves(w):
    lo = pltpu.bitcast(w << 16, f32).astype(bf16)
    hi = pltpu.bitcast(w & jnp.uint32(0xFFFF0000), f32).astype(bf16)
    return lo, hi


def _ln_route_kernel(x_ref, y_ref, gt_ref, g_ref, b_ref, sc_ref, sh_ref, wr_ref, br_ref,
                     xo_ref, uo_ref, comb_ref, id_ref, w_ref):
    z = DN_ALPHA * x_ref[...] + gt_ref[...] * y_ref[...]
    xn = _layernorm(z, g_ref[...], b_ref[...])
    xo_ref[...] = xn
    u = xn * (1.0 + sc_ref[...]) + sh_ref[...]
    uo_ref[...] = _pack_halves(u)
    logits = _dot3(u, wr_ref[...]) + br_ref[...]
    comb_ref[...], id_ref[...], w_ref[...] = _route(logits)


def _ln_route(x, y, gt, g, b, sc, sh, w_router, b_router, layer):
    s, d = x.shape
    tm = min(256, s)
    tile = pl.BlockSpec((tm, d), lambda i: (i, 0))
    row = pl.BlockSpec((1, d), lambda i: (0, 0))
    lrow = pl.BlockSpec((None, 1, d), lambda i: (layer, 0, 0))
    small = pl.BlockSpec((tm, LANES), lambda i: (i, 0))
    packed = pl.BlockSpec((tm, d // 2), lambda i: (i, 0))
    return pl.pallas_call(
        _ln_route_kernel,
        grid=(s // tm,),
        in_specs=[tile, tile, row, lrow, lrow, row, row,
                  pl.BlockSpec((None, d, N_EXPERTS), lambda i: (layer, 0, 0)),
                  pl.BlockSpec((None, 1, N_EXPERTS), lambda i: (layer, 0, 0))],
        out_specs=[tile, packed, pl.BlockSpec((tm, N_EXPERTS), lambda i: (i, 0)), small, small],
        out_shape=[jax.ShapeDtypeStruct((s, d), f32), jax.ShapeDtypeStruct((s, d // 2), jnp.uint32),
                   jax.ShapeDtypeStruct((s, N_EXPERTS), f32),
                   jax.ShapeDtypeStruct((s, LANES), jnp.int32),
                   jax.ShapeDtypeStruct((s, LANES), f32)],
        compiler_params=_params(("parallel",)),
        name="residual_ln_route",
    )(x, y, gt, g, b, sc, sh, w_router, b_router)


MOE_TM = 256
DISP_TM = 256
COMB_TM = 128
PAD_BITS = tuple(1 << p for p in reversed(range(int(math.log2(MOE_TM)))))


def _plan(id_tile, s):
    idx = id_tile[:, :TOP_K]
    experts = jnp.arange(N_EXPERTS, dtype=jnp.int32)
    onehot = idx[:, :, None] == experts[None, None, :]
    mask = jnp.any(onehot, axis=1).astype(jnp.int32)
    incl = jnp.cumsum(mask, axis=0)
    counts = incl[-1]
    padded = (counts + MOE_TM - 1) // MOE_TM * MOE_TM
    ends = jnp.cumsum(padded)
    starts = ends - padded
    dest_all = starts[None, :] + incl - mask
    dest = jnp.sum(jnp.where(onehot, dest_all[:, None, :], 0), axis=-1)
    n_tiles = (TOP_K * s) // MOE_TM + N_EXPERTS
    end_tiles = ends // MOE_TM
    n_used = end_tiles[-1]
    tiles = jnp.arange(n_tiles, dtype=jnp.int32)
    tile_expert = jnp.sum((jnp.minimum(tiles, n_used - 1)[:, None] >= end_tiles[None, :])
                          .astype(jnp.int32), axis=1)
    tile_expert = jnp.minimum(tile_expert, N_EXPERTS - 1)
    return (dest.astype(jnp.int32), tile_expert.astype(jnp.int32),
            n_used.reshape(1).astype(jnp.int32), (starts + counts).astype(jnp.int32),
            (padded - counts).astype(jnp.int32), n_tiles)


def _dispatch_kernel(pad_row_ref, pad_len_ref, used_ref, dest_ref, u_ref, us_ref, zero_ref, sem, zsem):
    i = pl.program_id(0)
    tm = u_ref.shape[0]

    for r in range(tm):
        for k in range(TOP_K):
            row = dest_ref[0, r * TOP_K + k]
            pltpu.make_async_copy(u_ref.at[pl.ds(r, 1)], us_ref.at[pl.ds(row, 1)], sem).start()

    @pl.when(i == pl.num_programs(0) - 1)
    def _():
        zero_ref[...] = jnp.zeros_like(zero_ref)

        def fill(wait, e, carry):
            row = pad_row_ref[e]
            n = pad_len_ref[e]
            for bit in reversed(PAD_BITS):
                size = bit if bit >= 8 else 1
                for part in range(bit // size):
                    @pl.when((n & bit) != 0)
                    def _():
                        at = row + part * size
                        if size >= 8:
                            at = pl.multiple_of(at, 8)
                        cp = pltpu.make_async_copy(zero_ref.at[pl.ds(0, size)],
                                                   us_ref.at[pl.ds(at, size)], zsem)
                        if wait:
                            cp.wait()
                        else:
                            cp.start()
                row = row + (n & bit)
            return carry

        def tail(wait, t, carry):
            tile = used_ref[0] + t

            @pl.when(tile < us_ref.shape[0] // MOE_TM)
            def _():
                for part in range(MOE_TM // zero_ref.shape[0]):
                    at = pl.multiple_of(tile * MOE_TM + part * zero_ref.shape[0], 8)
                    cp = pltpu.make_async_copy(
                        zero_ref, us_ref.at[pl.ds(at, zero_ref.shape[0])], zsem)
                    if wait:
                        cp.wait()
                    else:
                        cp.start()
            return carry

        for wait in (False, True):
            lax.fori_loop(0, N_EXPERTS, functools.partial(fill, wait), 0)
            lax.fori_loop(0, N_EXPERTS, functools.partial(tail, wait), 0)

    for _ in range(TOP_K):
        pltpu.make_async_copy(u_ref, us_ref.at[pl.ds(0, tm)], sem).wait()


def _dispatch(u, dest, pad_row, pad_len, n_used, n_tiles):
    s, d = u.shape
    tm = min(DISP_TM, s)
    dest3 = dest.reshape(s // tm, 1, tm * TOP_K)
    return pl.pallas_call(
        _dispatch_kernel,
        grid_spec=pltpu.PrefetchScalarGridSpec(
            num_scalar_prefetch=3,
            grid=(s // tm,),
            in_specs=[pl.BlockSpec((None, 1, tm * TOP_K), lambda i, *_: (i, 0, 0),
                                   memory_space=pltpu.SMEM),
                      pl.BlockSpec((tm, d), lambda i, *_: (i, 0))],
            out_specs=pl.BlockSpec(memory_space=pl.ANY),
            scratch_shapes=[pltpu.VMEM((MOE_TM // 2, d), u.dtype),
                            pltpu.SemaphoreType.DMA(()), pltpu.SemaphoreType.DMA(())]),
        out_shape=jax.ShapeDtypeStruct((n_tiles * MOE_TM, d), u.dtype),
        compiler_params=_params(("arbitrary",)),
        name="moe_dispatch",
    )(pad_row, pad_len, n_used, dest3, u)


def _experts_kernel(te_ref, nu_ref, x_ref, wgu_ref, bgu_ref, wd_ref, y_ref, wgu16_ref, wd16_ref):
    i = pl.program_id(0)
    new_expert = jnp.logical_or(i == 0, te_ref[i] != te_ref[jnp.maximum(i - 1, 0)])

    @pl.when(new_expert)
    def _():
        for r0 in range(0, wgu_ref.shape[0], W_SLAB):
            rs = slice(r0, r0 + W_SLAB)
            wgu16_ref[rs, :] = wgu_ref[rs, :].astype(bf16)
        wd16_ref[...] = wd_ref[...].astype(bf16)

    @pl.when(i < nu_ref[0])
    def _():
        lo, hi = _unpack_halves(x_ref[...])
        half = lo.shape[1]
        gu = _dot(lo, wgu16_ref[:half, :]) + _dot(hi, wgu16_ref[half:, :]) + bgu_ref[...]
        gate = jnp.minimum(gu[:, :D_EXPERT], SWIGLU_LIMIT)
        up = jnp.clip(gu[:, D_EXPERT:], -SWIGLU_LIMIT, SWIGLU_LIMIT)
        h = (up + 1.0) * gate * jax.nn.sigmoid(SWIGLU_ALPHA * gate)
        y_ref[...] = _dot(h.astype(bf16), wd16_ref[...])

    @pl.when(i >= nu_ref[0])
    def _():
        y_ref[...] = jnp.zeros_like(y_ref)


def _experts(us, tile_expert, n_used, w_gate_up, b_gate_up, w_down, layer):
    rows = us.shape[0]
    d = w_down.shape[-1]
    return pl.pallas_call(
        _experts_kernel,
        grid_spec=pltpu.PrefetchScalarGridSpec(
            num_scalar_prefetch=2,
            grid=(rows // MOE_TM,),
            in_specs=[pl.BlockSpec((MOE_TM, d // 2),
                                   lambda i, te, nu: (jnp.minimum(i, nu[0] - 1), 0)),
                      pl.BlockSpec((None, None, d, 2 * D_EXPERT),
                                   lambda i, te, nu: (layer, te[i], 0, 0)),
                      pl.BlockSpec((None, None, 1, 2 * D_EXPERT),
                                   lambda i, te, nu: (layer, te[i], 0, 0)),
                      pl.BlockSpec((None, None, D_EXPERT, d),
                                   lambda i, te, nu: (layer, te[i], 0, 0))],
            out_specs=pl.BlockSpec((MOE_TM, d), lambda i, te, nu: (i, 0)),
            scratch_shapes=[pltpu.VMEM((d, 2 * D_EXPERT), bf16),
                            pltpu.VMEM((D_EXPERT, d), bf16)]),
        out_shape=jax.ShapeDtypeStruct((rows, d), f32),
        compiler_params=_params(("arbitrary",)),
        name="moe_experts",
    )(tile_expert, n_used, us, w_gate_up, b_gate_up, w_down)


def _combine_ln_kernel(dest_ref, dnext_ref, x_ref, w_ref, comb_ref, bd_ref, gt_ref, g_ref, b_ref,
                       sc_ref, sh_ref, ys_ref, xo_ref, uo_ref, buf_ref, sem):
    i = pl.program_id(0)
    n = pl.num_programs(0)
    tm = x_ref.shape[0]
    slot = i % 2

    def gather(d_ref, to):
        for r in range(tm):
            for k in range(TOP_K):
                row = d_ref[0, r * TOP_K + k]
                pltpu.make_async_copy(ys_ref.at[pl.ds(row, 1)],
                                      buf_ref.at[to, k, pl.ds(r, 1)], sem.at[to]).start()

    def wait(at):
        for k in range(TOP_K):
            pltpu.make_async_copy(ys_ref.at[pl.ds(0, tm)], buf_ref.at[at, k], sem.at[at]).wait()

    @pl.when(i == 0)
    def _():
        gather(dest_ref, 0)

    wait(slot)
    gather(dnext_ref, 1 - slot)

    w_tile = w_ref[...]
    lane = lax.broadcasted_iota(jnp.int32, w_tile.shape, 1)
    y = _dot3(comb_ref[...], bd_ref[...])
    for k in range(TOP_K):
        w_k = jnp.sum(jnp.where(lane == k, w_tile, 0.0), axis=1, keepdims=True)
        y = y + w_k * buf_ref[slot, k]
    z = DN_ALPHA * x_ref[...] + gt_ref[...] * y
    xn = _layernorm(z, g_ref[...], b_ref[...])
    xo_ref[...] = xn
    uo_ref[...] = (xn * (1.0 + sc_ref[...]) + sh_ref[...]).astype(uo_ref.dtype)

    @pl.when(i == n - 1)
    def _():
        wait(1 - slot)


def _combine_ln(x, ys, dest, w_tile, comb, b_down, gt, g, b, sc, sh, layer):
    s, d = x.shape
    tm = min(COMB_TM, s)
    steps = s // tm
    dest3 = dest.reshape(steps, 1, tm * TOP_K)
    tile = pl.BlockSpec((tm, d), lambda i: (i, 0))
    row = pl.BlockSpec((1, d), lambda i: (0, 0))
    lrow = pl.BlockSpec((None, 1, d), lambda i: (layer, 0, 0))
    return pl.pallas_call(
        _combine_ln_kernel,
        grid=(steps,),
        in_specs=[pl.BlockSpec((None, 1, tm * TOP_K), lambda i: (i, 0, 0), memory_space=pltpu.SMEM),
                  pl.BlockSpec((None, 1, tm * TOP_K),
                               lambda i: (jnp.minimum(i + 1, steps - 1), 0, 0),
                               memory_space=pltpu.SMEM),
                  tile,
                  pl.BlockSpec((tm, LANES), lambda i: (i, 0)),
                  pl.BlockSpec((tm, N_EXPERTS), lambda i: (i, 0)),
                  pl.BlockSpec((None, N_EXPERTS, d), lambda i: (layer, 0, 0)),
                  row, lrow, lrow, row, row,
                  pl.BlockSpec(memory_space=pl.ANY)],
        out_specs=[tile, tile],
        out_shape=[jax.ShapeDtypeStruct((s, d), f32), jax.ShapeDtypeStruct((s, d), bf16)],
        scratch_shapes=[pltpu.VMEM((2, TOP_K, tm, d), f32), pltpu.SemaphoreType.DMA((2,))],
        compiler_params=_params(("arbitrary",)),
        name="moe_combine_ln",
    )(dest3, dest3, x, w_tile, comb, b_down, gt, g, b, sc, sh, ys)


def _silu(x):
    return x * jax.nn.sigmoid(x)


def _l2norm(x):
    return x * lax.rsqrt(jnp.sum(x * x, axis=-1, keepdims=True) + NORM_EPS)


def _ba_kernel(u_ref, w_ref, alog_ref, dt_ref, o_ref):
    tm = u_ref.shape[0]
    ba = _dot_nt(u_ref[...], w_ref[...].astype(bf16))
    o_ref[:, :LANES] = jax.nn.sigmoid(ba)
    xa = ba + dt_ref[...]
    softplus = jnp.maximum(xa, 0.0) + jnp.log(1.0 + jnp.exp(-jnp.abs(xa)))
    g = -jnp.exp(alog_ref[...]) * softplus
    row_i = lax.broadcasted_iota(jnp.int32, (CHUNK, CHUNK), 0)
    col_i = lax.broadcasted_iota(jnp.int32, (CHUNK, CHUNK), 1)
    tril = jnp.where(row_i >= col_i, 1.0, 0.0).astype(bf16)
    for c in range(tm // CHUNK):
        r = slice(c * CHUNK, (c + 1) * CHUNK)
        g1 = g[r].astype(bf16)
        r1 = g[r] - g1.astype(f32)
        g2 = r1.astype(bf16)
        g3 = (r1 - g2.astype(f32)).astype(bf16)
        o_ref[r, LANES:] = _dot(tril, g1) + _dot(tril, g2) + _dot(tril, g3)


def _ba(u, w_in_t, alog, dtb, layer):
    s, d = u.shape
    tm = min(1024, s)
    lrow = pl.BlockSpec((None, 1, LANES), lambda i: (layer, 0, 0))
    return pl.pallas_call(
        _ba_kernel,
        grid=(s // tm,),
        in_specs=[pl.BlockSpec((tm, d), lambda i: (i, 0)),
                  pl.BlockSpec((None, LANES, d), lambda i: (layer, OFF_BA // LANES, 0)),
                  lrow, lrow],
        out_specs=pl.BlockSpec((tm, 2 * LANES), lambda i: (i, 0)),
        out_shape=jax.ShapeDtypeStruct((s, 2 * LANES), f32),
        compiler_params=_params(("parallel",)),
        name="beta_decay",
    )(u, w_in_t, alog, dtb)


def _delta_kernel(heads,
                  q_ref, k_ref, v_ref, z_ref, qh_ref, kh_ref, vh_ref, beta_ref, cum_ref,
                  cq_ref, ck_ref, cv_ref, na_ref,
                  o_ref, state_ref, xs_ref, cumt_ref):
    hp = pl.program_id(0)
    i = pl.program_id(1)

    @pl.when(i == 0)
    def _():
        state_ref[...] = jnp.zeros_like(state_ref)

    first = i == 0
    row_i = lax.broadcasted_iota(jnp.int32, (CHUNK, CHUNK), 0)
    col_i = lax.broadcasted_iota(jnp.int32, (CHUNK, CHUNK), 1)
    causal = row_i >= col_i
    strict = row_i > col_i
    eye = jnp.where(row_i == col_i, 1.0, 0.0)
    merge_masks = []
    for shift in range(int(math.log2(CHUNK))):
        rb = row_i >> shift
        merge_masks.append(jnp.logical_and((rb & 1) == 1, (col_i >> shift) == rb - 1))

    def conv_silu(x_ref, halo_ref, cw_ref):
        xs_ref[0:8, :] = jnp.where(first, 0.0, halo_ref[...])
        xs_ref[8:8 + CHUNK, :] = x_ref[...]
        acc = cw_ref[CONV_K - 1:CONV_K, :] * xs_ref[8:8 + CHUNK, :]
        for back in range(1, CONV_K):
            tap = CONV_K - 1 - back
            acc = acc + cw_ref[tap:tap + 1, :] * xs_ref[8 - back:8 - back + CHUNK, :]
        return _silu(acc)

    q_all = conv_silu(q_ref, qh_ref, cq_ref)
    k_all = conv_silu(k_ref, kh_ref, ck_ref)
    v_all = conv_silu(v_ref, vh_ref, cv_ref)
    beta_t = beta_ref[...]
    cum_tile = cum_ref[...]
    cumt_ref[...] = cum_tile.T

    hs = range(heads)
    sls = [slice(j * LANES, (j + 1) * LANES) for j in hs]
    q, k, v, beta_b, cum, decay = [], [], [], [], [], []
    for j in hs:
        head = hp * heads + j
        q.append(_l2norm(q_all[:, sls[j]]) * (DKA ** -0.5))
        k.append(_l2norm(k_all[:, sls[j]]))
        v.append(v_all[:, sls[j]])
        beta_b.append(jnp.broadcast_to(
            jnp.sum(jnp.where(col_i == head, beta_t, 0.0), axis=1, keepdims=True), (CHUNK, CHUNK)))
        cum.append(jnp.broadcast_to(
            jnp.sum(jnp.where(col_i == head + HA, cum_tile, 0.0), axis=1, keepdims=True),
            (CHUNK, CHUNK)))
        cum_t = jnp.broadcast_to(cumt_ref[pl.ds(head + HA, 1), :], (CHUNK, CHUNK))
        decay.append(jnp.exp(jnp.where(causal, cum[j] - cum_t, NEG_BIG)))
    k16 = [k[j].astype(bf16) for j in hs]
    kb = [k[j] * beta_b[j] for j in hs]
    lmat = [jnp.where(strict, _dot_nt(kb[j].astype(bf16), k16[j]) * decay[j], 0.0) for j in hs]
    intra = [jnp.where(causal, _dot_nt(q[j].astype(bf16), k16[j]) * decay[j], 0.0).astype(bf16)
             for j in hs]
    tn = [jnp.where(merge_masks[0], -lmat[j], 0.0) for j in hs]
    for mask in merge_masks[1:]:
        t_inv = [(eye + tn[j]).astype(bf16) for j in hs]
        half = [_dot(t_inv[j], jnp.where(mask, lmat[j], 0.0).astype(bf16)).astype(bf16) for j in hs]
        tn = [tn[j] - _dot(half[j], t_inv[j]) for j in hs]
    e_g = [jnp.exp(cum[j]) for j in hs]
    g_last = [cum[j][CHUNK - 1:CHUNK, :] for j in hs]
    rhs = [jnp.concatenate([v[j] * beta_b[j], kb[j] * e_g[j]], axis=1) for j in hs]
    uw = [rhs[j] + _dot(tn[j].astype(bf16), rhs[j].astype(bf16)) for j in hs]
    k_dec_t = [(k[j] * jnp.exp(g_last[j] - cum[j])).T.astype(bf16) for j in hs]
    state = [state_ref[j] for j in hs]
    ws = [_dot(jnp.concatenate([uw[j][:, DVA:], q[j] * e_g[j]], axis=0).astype(bf16),
               state[j].astype(bf16)) for j in hs]
    v16 = [(uw[j][:, :DVA] - ws[j][:CHUNK]).astype(bf16) for j in hs]
    o_c = [ws[j][CHUNK:] + _dot(intra[j], v16[j]) for j in hs]
    for j in hs:
        state_ref[j] = state[j] * jnp.exp(g_last[j]) + _dot(k_dec_t[j], v16[j])
    for j in hs:
        o_n = o_c[j] * lax.rsqrt(jnp.mean(o_c[j] * o_c[j], axis=-1, keepdims=True) + NORM_EPS)
        o_ref[:, sls[j]] = (o_n * na_ref[...] * _silu(z_ref[:, sls[j]])).astype(o_ref.dtype)


def _delta(proj_a, bg, conv_w, norm_a, layer):
    s = proj_a.shape[0]
    heads = 16
    hw = heads * LANES
    halo_blocks = CHUNK // 8

    def tile(col0):
        return pl.BlockSpec((CHUNK, hw), lambda hp, i: (i, hp + col0 // hw))

    def halo(col0):
        return pl.BlockSpec(
            (8, hw), lambda hp, i: (jnp.maximum(i * halo_blocks - 1, 0), hp + col0 // hw))

    def cw(col0):
        return pl.BlockSpec((None, CONV_K, hw), lambda hp, i: (layer, 0, hp + col0 // hw))

    return pl.pallas_call(
        functools.partial(_delta_kernel, heads),
        grid=(WA // hw, s // CHUNK),
        in_specs=[tile(0), tile(WA), tile(2 * WA), tile(2 * WA + WVA),
                  halo(0), halo(WA), halo(2 * WA),
                  pl.BlockSpec((CHUNK, LANES), lambda hp, i: (i, 0)),
                  pl.BlockSpec((CHUNK, LANES), lambda hp, i: (i, 1)),
                  cw(0), cw(WA), cw(2 * WA),
                  pl.BlockSpec((None, 1, LANES), lambda hp, i: (layer, 0, 0))],
        out_specs=pl.BlockSpec((CHUNK, hw), lambda hp, i: (i, hp)),
        out_shape=jax.ShapeDtypeStruct((s, WVA), bf16),
        scratch_shapes=[pltpu.VMEM((heads, DKA, DVA), f32),
                        pltpu.VMEM((CHUNK + 8, hw), f32),
                        pltpu.VMEM((LANES, CHUNK), f32)],
        compiler_params=_params(("parallel", "arbitrary")),
        name="gated_delta",
    )(proj_a, proj_a, proj_a, proj_a, proj_a, proj_a, proj_a, bg, bg,
      conv_w, conv_w, conv_w, norm_a)


def _attn_kernel(layer, q_ref, kp_ref, kc_ref, vp_ref, vc_ref, bm_ref, sink_ref, o_ref):
    n = pl.program_id(0)
    lane = lax.broadcasted_iota(jnp.int32, (2 * BLK, LANES), 1)
    low = lane < DHB
    key_j = lax.broadcasted_iota(jnp.int32, (BLK, 2 * BLK), 1)
    no_prev = jnp.logical_and(n == 0, key_j < BLK)
    scale = DHB ** -0.5
    groups_per_tile = LANES // DHB
    pairs_per_group = HB // HB_KV // 2
    k_cat, v_cat = [], []
    for t in range(WKVB // LANES):
        sl = slice(t * LANES, (t + 1) * LANES)
        k_t = jnp.concatenate([kp_ref[:, sl], kc_ref[:, sl]], axis=0)
        v_t = jnp.concatenate([vp_ref[:, sl], vc_ref[:, sl]], axis=0)
        for gg in range(groups_per_tile):
            mine = low if gg == 0 else jnp.logical_not(low)
            k_m = jnp.where(mine, k_t, 0.0)
            v_m = jnp.where(mine, v_t, 0.0)
            k_2 = k_m + pltpu.roll(k_m, DHB, axis=1)
            v_2 = v_m + pltpu.roll(v_m, DHB, axis=1)
            k_cat.append(jnp.concatenate([jnp.where(low, k_2, 0.0),
                                          jnp.where(low, 0.0, k_2)], axis=0).astype(bf16))
            v_cat.append(jnp.concatenate([jnp.where(low, v_2, 0.0),
                                          jnp.where(low, 0.0, v_2)], axis=0).astype(bf16))
    pairs = range(HB // 2)
    qls = [slice(p * LANES, (p + 1) * LANES) for p in pairs]
    scores = [_dot_nt((q_ref[:, qls[p]] * scale).astype(bf16), k_cat[p // pairs_per_group])
              for p in pairs]
    probs = []
    for p in pairs:
        halves = []
        for hh in range(2):
            head = 2 * p + hh
            sh = scores[p][:, hh * 2 * BLK:(hh + 1) * 2 * BLK] + bm_ref[head]
            sh = jnp.where(no_prev, NEG_BIG, sh)
            sink = sink_ref[layer, head]
            mx = jnp.maximum(jnp.max(sh, axis=-1, keepdims=True), sink)
            e = jnp.exp(sh - mx)
            den = jnp.sum(e, axis=-1, keepdims=True) + jnp.exp(sink - mx)
            halves.append((e * (1.0 / den)).astype(bf16))
        probs.append(jnp.concatenate(halves, axis=1))
    for p in pairs:
        o_ref[:, qls[p]] = _dot(probs[p], v_cat[p // pairs_per_group]).astype(o_ref.dtype)


def _attn(proj_b, bias_mask, sinks, layer):
    s = proj_b.shape[0]
    kb0 = WQB // WKVB
    return pl.pallas_call(
        functools.partial(_attn_kernel, layer),
        grid=(s // BLK,),
        in_specs=[pl.BlockSpec((BLK, WQB), lambda n: (n, 0)),
                  pl.BlockSpec((BLK, WKVB), lambda n: (jnp.maximum(n - 1, 0), kb0)),
                  pl.BlockSpec((BLK, WKVB), lambda n: (n, kb0)),
                  pl.BlockSpec((BLK, WKVB), lambda n: (jnp.maximum(n - 1, 0), kb0 + 1)),
                  pl.BlockSpec((BLK, WKVB), lambda n: (n, kb0 + 1)),
                  pl.BlockSpec((HB, BLK, 2 * BLK), lambda n: (0, 0, 0)),
                  pl.BlockSpec(memory_space=pltpu.SMEM)],
        out_specs=pl.BlockSpec((BLK, WQB), lambda n: (n, 0)),
        out_shape=jax.ShapeDtypeStruct((s, WQB), bf16),
        compiler_params=_params(("parallel",)),
        name="swa_attention",
    )(proj_b, proj_b, proj_b, proj_b, proj_b, bias_mask, sinks)


def _t5_bucket(d):
    n_exact = N_BUCKETS // 2
    df = jnp.maximum(d, 1).astype(f32)
    large = n_exact + (jnp.log(df / n_exact) / math.log(MAX_DIST / n_exact)
                       * (N_BUCKETS - n_exact)).astype(jnp.int32)
    large = jnp.minimum(large, N_BUCKETS - 1)
    return jnp.where(d < n_exact, d, large)


def _bias_kernel(rbt_ref, bucket_ref, inside_ref, o_ref):
    n = bucket_ref.shape[1]
    rows = lax.broadcasted_iota(jnp.int32, (N_BUCKETS, n), 0)
    onehot = jnp.where(rows == bucket_ref[...], 1.0, 0.0).astype(bf16)
    r = rbt_ref[...]
    r1 = r.astype(bf16)
    rem = r - r1.astype(f32)
    r2 = rem.astype(bf16)
    r3 = (rem - r2.astype(f32)).astype(bf16)
    table = _dot(r1, onehot) + _dot(r2, onehot) + _dot(r3, onehot)
    o_ref[...] = jnp.where(inside_ref[...] != 0, table, NEG_BIG)


def _band_bias(rel_bias):
    i = jnp.arange(BLK)[:, None]
    j = jnp.arange(2 * BLK)[None, :]
    d = i + BLK - j
    inside = ((d >= 0) & (d < WINDOW)).astype(jnp.int32).reshape(1, -1)
    bucket = _t5_bucket(jnp.clip(d, 0, MAX_DIST - 1)).astype(jnp.int32).reshape(1, -1)
    n = bucket.shape[1]
    tn = 4096
    flat = pl.pallas_call(
        _bias_kernel,
        grid=(n // tn,),
        in_specs=[pl.BlockSpec((HB, N_BUCKETS), lambda c: (0, 0)),
                  pl.BlockSpec((1, tn), lambda c: (0, c)),
                  pl.BlockSpec((1, tn), lambda c: (0, c))],
        out_specs=pl.BlockSpec((HB, tn), lambda c: (0, c)),
        out_shape=jax.ShapeDtypeStruct((HB, n), f32),
        compiler_params=_params(("parallel",)),
        name="band_bias",
    )(rel_bias.astype(f32).T, bucket, inside)
    return flat.reshape(HB, BLK, 2 * BLK)


def kernel(x, c, w_ada, b_ada, ada_table, rel_bias, w_in, conv_w, a_log, dt_bias, norm_a, sinks,
           w_up_a, w_up_b, w_o, ln1_g, ln1_b, w_router, b_router, w_gate_up, b_gate_up, w_down,
           b_down, ln2_g, ln2_b):
    bsz, s, d = x.shape
    assert bsz == 1 and d == D_MODEL and s % BLK == 0
    depth = w_in.shape[0]
    xs = x.reshape(s, d)

    mod_base = _ada(jnp.broadcast_to(c, (8, d)), w_ada, b_ada)[0]
    mod = mod_base.reshape(1, N_MOD, d) + ada_table

    b_gu = b_gate_up.reshape(depth, N_EXPERTS, 1, 2 * D_EXPERT)
    b_rt = b_router.reshape(depth, 1, N_EXPERTS)
    pad_h = ((0, 0), (HA, LANES - 2 * HA))
    alog = jnp.pad(a_log, pad_h).reshape(depth, 1, LANES)
    dtb = jnp.pad(dt_bias, pad_h).reshape(depth, 1, LANES)
    na = norm_a.reshape(depth, 1, DVA)
    g1 = ln1_g.reshape(depth, 1, d)
    b1 = ln1_b.reshape(depth, 1, d)
    g2 = ln2_g.reshape(depth, 1, d)
    b2 = ln2_b.reshape(depth, 1, d)
    bias_mask = _band_bias(rel_bias)
    w_in_t = jnp.swapaxes(w_in, 1, 2)

    def mrow(layer, idx):
        return mod[layer, idx].reshape(1, d)

    u = _modulate(xs, mrow(0, 1), mrow(0, 0))
    for layer in range(depth):
        proj_a = _matmul(u, w_in_t, layer, 0, W_PROJ_A, f32, 1024, 512, "in_proj_a", True)
        proj_b = _matmul(u, w_in_t, layer, OFF_B, W_PROJ_B, f32, 1024, 512, "in_proj_b", True)
        bg = _ba(u, w_in_t, alog, dtb, layer)
        o_a = _delta(proj_a, bg, conv_w, na, layer)
        o_b = _attn(proj_b, bias_mask, sinks, layer)
        merged = _merge(o_a, o_b, w_up_a, w_up_b, proj_b, layer)
        y = _matmul(merged, w_o, layer, 0, d, f32, 1024, 512, "out_proj")
        xs, u_moe, comb, id_tile, w_tile = _ln_route(
            xs, y, mrow(layer, 2), g1, b1, mrow(layer, 4), mrow(layer, 3), w_router, b_rt, layer)
        dest, tile_expert, n_used, pad_row, pad_len, n_tiles = _plan(id_tile, s)
        us = _dispatch(u_moe, dest, pad_row, pad_len, n_used, n_tiles)
        ys = _experts(us, tile_expert, n_used, w_gate_up, b_gu, w_down, layer)
        nxt = min(layer + 1, depth - 1)
        xs, u = _combine_ln(xs, ys, dest, w_tile, comb, b_down, mrow(layer, 5), g2, b2,
                            mrow(nxt, 1), mrow(nxt, 0), layer)
    return xs.reshape(bsz, s, d)
```

```python
import functools
import math

import jax
import jax.numpy as jnp
import numpy as np
from jax import lax
from jax.experimental import pallas as pl
from jax.experimental.pallas import tpu as pltpu

D_MODEL = 4096
DEPTH = 4
HA = 16
DKA = 128
DVA = 128
CONV_K = 4
HB = 32
HB_KV = 8
DHB = 64
WINDOW = 128
BLK = 128
N_BUCKETS = 32
MAX_DIST = WINDOW
N_EXPERTS = 32
TOP_K = 4
D_EXPERT = 256
SWIGLU_LIMIT = 7.0
SWIGLU_ALPHA = 1.702
DN_ALPHA = (2 * DEPTH) ** 0.25
LN_EPS = 1e-5
NORM_EPS = 1e-6
N_MOD = 6
WA = HA * DKA
WVA = HA * DVA
WQB = HB * DHB
WKVB = HB_KV * DHB
OFF_BA = 2 * WA + 2 * WVA
OFF_B = OFF_BA + 2 * HA
W_PROJ_A = OFF_BA
W_PROJ_B = WQB + 2 * WKVB + 2 * D_MODEL

LANES = 128
CHUNK = 128
VMEM_LIMIT = 56 * 1024 * 1024
NEG_BIG = -1e30

bf16 = jnp.bfloat16
f32 = jnp.float32


def _dot(a, b):
    return jnp.dot(a, b, preferred_element_type=f32)


def _dot_nt(a, b):
    return lax.dot_general(a, b, (((1,), (1,)), ((), ())), preferred_element_type=f32)


def _split2(x):
    hi = x.astype(bf16)
    lo = (x - hi.astype(f32)).astype(bf16)
    return hi, lo


def _dot3(a, b):
    ah, al = _split2(a)
    bh, bl = _split2(b)
    return _dot(ah, bh) + _dot(ah, bl) + _dot(al, bh)


def _params(sem):
    return pltpu.CompilerParams(dimension_semantics=sem, vmem_limit_bytes=VMEM_LIMIT)


def _ada_kernel(c_ref, w_ref, b_ref, o_ref):
    c = c_ref[...]
    a = c * jax.nn.sigmoid(c)
    o_ref[...] = _dot3(a, w_ref[...]) + b_ref[...]


def _ada(c8, w_ada, b_ada):
    n = w_ada.shape[1]
    tn = 512
    return pl.pallas_call(
        _ada_kernel,
        grid=(n // tn,),
        in_specs=[pl.BlockSpec((8, D_MODEL), lambda j: (0, 0)),
                  pl.BlockSpec((D_MODEL, tn), lambda j: (0, j)),
                  pl.BlockSpec((1, tn), lambda j: (0, j))],
        out_specs=pl.BlockSpec((8, tn), lambda j: (0, j)),
        out_shape=jax.ShapeDtypeStruct((8, n), f32),
        compiler_params=_params(("parallel",)),
        name="ada_mod",
    )(c8, w_ada, b_ada.reshape(1, n))


def _modulate_kernel(x_ref, sc_ref, sh_ref, u_ref):
    u_ref[...] = (x_ref[...] * (1.0 + sc_ref[...]) + sh_ref[...]).astype(u_ref.dtype)


def _modulate(x, sc, sh):
    s, d = x.shape
    tm = min(256, s)
    row = pl.BlockSpec((1, d), lambda i: (0, 0))
    return pl.pallas_call(
        _modulate_kernel,
        grid=(s // tm,),
        in_specs=[pl.BlockSpec((tm, d), lambda i: (i, 0)), row, row],
        out_specs=pl.BlockSpec((tm, d), lambda i: (i, 0)),
        out_shape=jax.ShapeDtypeStruct((s, d), bf16),
        compiler_params=_params(("parallel",)),
        name="modulate",
    )(x, sc, sh)


W_SLAB = 512


def _mm_kernel(transposed, a_ref, w_ref, o_ref, wb_ref):
    first = pl.program_id(1) == 0

    @pl.when(first)
    def _():
        if transposed:
            o_ref[...] = _dot_nt(a_ref[...], w_ref[0].astype(bf16)).astype(o_ref.dtype)
            for r0 in range(0, wb_ref.shape[0], W_SLAB):
                rs = slice(r0, r0 + W_SLAB)
                wb_ref[rs, :] = w_ref[0, :, rs].T.astype(bf16)
        else:
            w16 = w_ref[...].astype(bf16)
            o_ref[...] = _dot(a_ref[...], w16).astype(o_ref.dtype)
            wb_ref[...] = w16

    @pl.when(jnp.logical_not(first))
    def _():
        o_ref[...] = _dot(a_ref[...], wb_ref[...]).astype(o_ref.dtype)


def _matmul(a, w, layer, col0, ncols, out_dtype, tm, tn, name, transposed=False):
    m, k = a.shape
    tm = min(tm, m)
    assert ncols % tn == 0 and k % W_SLAB == 0
    if transposed:
        assert col0 % 8 == 0
        w_spec = pl.BlockSpec((pl.Element(1), pl.Element(tn), pl.Element(k)),
                              lambda j, i: (layer, pl.multiple_of(col0 + j * tn, 8), 0))
    else:
        assert col0 % tn == 0
        w_spec = pl.BlockSpec((None, k, tn), lambda j, i: (layer, 0, j + col0 // tn))
    return pl.pallas_call(
        functools.partial(_mm_kernel, transposed),
        grid=(ncols // tn, m // tm),
        in_specs=[pl.BlockSpec((tm, k), lambda j, i: (i, 0)), w_spec],
        out_specs=pl.BlockSpec((tm, tn), lambda j, i: (i, j)),
        out_shape=jax.ShapeDtypeStruct((m, ncols), out_dtype),
        scratch_shapes=[pltpu.VMEM((k, tn), bf16)],
        compiler_params=_params(("arbitrary", "arbitrary")),
        name=name,
    )(a, w)


def _merge_kernel(oa_ref, ob_ref, wa_ref, wb_ref, ga_ref, gb_ref, o_ref, wa16_ref, wb16_ref):
    @pl.when(pl.program_id(1) == 0)
    def _():
        wa16_ref[...] = wa_ref[...].astype(bf16)
        wb16_ref[...] = wb_ref[...].astype(bf16)

    ya = _dot(oa_ref[...], wa16_ref[...])
    yb = _dot(ob_ref[...], wb16_ref[...])
    o_ref[...] = (jax.nn.sigmoid(ga_ref[...]) * ya
                  + jax.nn.sigmoid(gb_ref[...]) * yb).astype(o_ref.dtype)


def _merge(oa, ob, w_up_a, w_up_b, proj_b, layer):
    s = oa.shape[0]
    tm = min(1024, s)
    tn = 512
    ga0 = (WQB + 2 * WKVB) // tn
    gb0 = ga0 + D_MODEL // tn
    return pl.pallas_call(
        _merge_kernel,
        grid=(D_MODEL // tn, s // tm),
        in_specs=[pl.BlockSpec((tm, WVA), lambda j, i: (i, 0)),
                  pl.BlockSpec((tm, WQB), lambda j, i: (i, 0)),
                  pl.BlockSpec((None, WVA, tn), lambda j, i: (layer, 0, j)),
                  pl.BlockSpec((None, WQB, tn), lambda j, i: (layer, 0, j)),
                  pl.BlockSpec((tm, tn), lambda j, i: (i, j + ga0)),
                  pl.BlockSpec((tm, tn), lambda j, i: (i, j + gb0))],
        out_specs=pl.BlockSpec((tm, tn), lambda j, i: (i, j)),
        out_shape=jax.ShapeDtypeStruct((s, D_MODEL), bf16),
        scratch_shapes=[pltpu.VMEM((WVA, tn), bf16), pltpu.VMEM((WQB, tn), bf16)],
        compiler_params=_params(("arbitrary", "arbitrary")),
        name="gated_merge",
    )(oa, ob, w_up_a, w_up_b, proj_b, proj_b)


def _layernorm(z, g, b):
    mu = jnp.mean(z, axis=-1, keepdims=True)
    zc = z - mu
    var = jnp.mean(zc * zc, axis=-1, keepdims=True)
    return zc * lax.rsqrt(var + LN_EPS) * g + b


def _route(logits):
    tm, n_e = logits.shape
    lane = lax.broadcasted_iota(jnp.int32, logits.shape, 1)
    slot = lax.broadcasted_iota(jnp.int32, (tm, LANES), 1)
    work = logits
    ids, vals = [], []
    for _ in range(TOP_K):
        m = jnp.max(work, axis=-1, keepdims=True)
        idx = jnp.min(jnp.where(work == m, lane, n_e), axis=-1, keepdims=True)
        work = jnp.where(lane == idx, -jnp.inf, work)
        ids.append(idx)
        vals.append(m)
    es = [jnp.exp(v - vals[0]) for v in vals]
    inv = 1.0 / functools.reduce(lambda p, q: p + q, es)
    comb = jnp.zeros(logits.shape, f32)
    id_tile = jnp.zeros((tm, LANES), jnp.int32)
    w_tile = jnp.zeros((tm, LANES), f32)
    for r in range(TOP_K):
        w = es[r] * inv
        comb = jnp.where(lane == ids[r], w, comb)
        id_tile = jnp.where(slot == r, ids[r], id_tile)
        w_tile = jnp.where(slot == r, w, w_tile)
    return comb, id_tile, w_tile


def _pack_halves(x):
    n = x.shape[1] // 2
    lo = pltpu.bitcast(x[:, :n].astype(bf16).astype(f32), jnp.uint32)
    hi = pltpu.bitcast(x[:, n:].astype(bf16).astype(f32), jnp.uint32)
    return (lo >> 16) | (hi & jnp.uint32(0xFFFF0000))


def _unpack_halves(w):
    lo = pltpu.bitcast(w << 16, f32).astype(bf16)
    hi = pltpu.bitcast(w & jnp.uint32(0xFFFF0000), f32).astype(bf16)
    return lo, hi


def _unpack_halves_f32(w):
    return pltpu.bitcast(w << 16, f32), pltpu.bitcast(w & jnp.uint32(0xFFFF0000), f32)


def _ln_route_kernel(x_ref, y_ref, gt_ref, g_ref, b_ref, sc_ref, sh_ref, wr_ref, br_ref,
                     xo_ref, uo_ref, comb_ref, id_ref, w_ref):
    z = DN_ALPHA * x_ref[...] + gt_ref[...] * y_ref[...].astype(f32)
    xn = _layernorm(z, g_ref[...], b_ref[...])
    xo_ref[...] = xn
    u = xn * (1.0 + sc_ref[...]) + sh_ref[...]
    uo_ref[...] = _pack_halves(u)
    logits = _dot3(u, wr_ref[...]) + br_ref[...]
    comb_ref[...], id_ref[...], w_ref[...] = _route(logits)


def _ln_route(x, y, gt, g, b, sc, sh, w_router, b_router, layer):
    s, d = x.shape
    tm = min(256, s)
    tile = pl.BlockSpec((tm, d), lambda i: (i, 0))
    row = pl.BlockSpec((1, d), lambda i: (0, 0))
    lrow = pl.BlockSpec((None, 1, d), lambda i: (layer, 0, 0))
    small = pl.BlockSpec((tm, LANES), lambda i: (i, 0))
    packed = pl.BlockSpec((tm, d // 2), lambda i: (i, 0))
    return pl.pallas_call(
        _ln_route_kernel,
        grid=(s // tm,),
        in_specs=[tile, tile, row, lrow, lrow, row, row,
                  pl.BlockSpec((None, d, N_EXPERTS), lambda i: (layer, 0, 0)),
                  pl.BlockSpec((None, 1, N_EXPERTS), lambda i: (layer, 0, 0))],
        out_specs=[tile, packed, pl.BlockSpec((tm, N_EXPERTS), lambda i: (i, 0)), small, small],
        out_shape=[jax.ShapeDtypeStruct((s, d), f32), jax.ShapeDtypeStruct((s, d // 2), jnp.uint32),
                   jax.ShapeDtypeStruct((s, N_EXPERTS), f32),
                   jax.ShapeDtypeStruct((s, LANES), jnp.int32),
                   jax.ShapeDtypeStruct((s, LANES), f32)],
        compiler_params=_params(("parallel",)),
        name="residual_ln_route",
    )(x, y, gt, g, b, sc, sh, w_router, b_router)


MOE_TM = 256
DISP_TM = 256
COMB_TM = 128
PAD_BITS = tuple(1 << p for p in reversed(range(int(math.log2(MOE_TM)))))


def _plan(id_tile, s):
    idx = id_tile[:, :TOP_K]
    experts = jnp.arange(N_EXPERTS, dtype=jnp.int32)
    onehot = idx[:, :, None] == experts[None, None, :]
    mask = jnp.any(onehot, axis=1).astype(jnp.int32)
    incl = jnp.cumsum(mask, axis=0)
    counts = incl[-1]
    padded = (counts + MOE_TM - 1) // MOE_TM * MOE_TM
    ends = jnp.cumsum(padded)
    starts = ends - padded
    dest_all = starts[None, :] + incl - mask
    dest = jnp.sum(jnp.where(onehot, dest_all[:, None, :], 0), axis=-1)
    n_tiles = (TOP_K * s) // MOE_TM + N_EXPERTS
    end_tiles = ends // MOE_TM
    n_used = end_tiles[-1]
    tiles = jnp.arange(n_tiles, dtype=jnp.int32)
    tile_expert = jnp.sum((jnp.minimum(tiles, n_used - 1)[:, None] >= end_tiles[None, :])
                          .astype(jnp.int32), axis=1)
    tile_expert = jnp.minimum(tile_expert, N_EXPERTS - 1)
    return (dest.astype(jnp.int32), tile_expert.astype(jnp.int32),
            n_used.reshape(1).astype(jnp.int32), (starts + counts).astype(jnp.int32),
            (padded - counts).astype(jnp.int32), n_tiles)


def _dispatch_kernel(pad_row_ref, pad_len_ref, used_ref, dest_ref, u_ref, us_ref, zero_ref, sem, zsem):
    i = pl.program_id(0)
    tm = u_ref.shape[0]

    for r in range(tm):
        for k in range(TOP_K):
            row = dest_ref[0, r * TOP_K + k]
            pltpu.make_async_copy(u_ref.at[pl.ds(r, 1)], us_ref.at[pl.ds(row, 1)], sem).start()

    @pl.when(i == pl.num_programs(0) - 1)
    def _():
        zero_ref[...] = jnp.zeros_like(zero_ref)

        def fill(wait, e, carry):
            row = pad_row_ref[e]
            n = pad_len_ref[e]
            for bit in reversed(PAD_BITS):
                size = bit if bit >= 8 else 1
                for part in range(bit // size):
                    @pl.when((n & bit) != 0)
                    def _():
                        at = row + part * size
                        if size >= 8:
                            at = pl.multiple_of(at, 8)
                        cp = pltpu.make_async_copy(zero_ref.at[pl.ds(0, size)],
                                                   us_ref.at[pl.ds(at, size)], zsem)
                        if wait:
                            cp.wait()
                        else:
                            cp.start()
                row = row + (n & bit)
            return carry

        def tail(wait, t, carry):
            tile = used_ref[0] + t

            @pl.when(tile < us_ref.shape[0] // MOE_TM)
            def _():
                for part in range(MOE_TM // zero_ref.shape[0]):
                    at = pl.multiple_of(tile * MOE_TM + part * zero_ref.shape[0], 8)
                    cp = pltpu.make_async_copy(
                        zero_ref, us_ref.at[pl.ds(at, zero_ref.shape[0])], zsem)
                    if wait:
                        cp.wait()
                    else:
                        cp.start()
            return carry

        for wait in (False, True):
            lax.fori_loop(0, N_EXPERTS, functools.partial(fill, wait), 0)
            lax.fori_loop(0, N_EXPERTS, functools.partial(tail, wait), 0)

    for _ in range(TOP_K):
        pltpu.make_async_copy(u_ref, us_ref.at[pl.ds(0, tm)], sem).wait()


def _dispatch(u, dest, pad_row, pad_len, n_used, n_tiles):
    s, d = u.shape
    tm = min(DISP_TM, s)
    dest3 = dest.reshape(s // tm, 1, tm * TOP_K)
    return pl.pallas_call(
        _dispatch_kernel,
        grid_spec=pltpu.PrefetchScalarGridSpec(
            num_scalar_prefetch=3,
            grid=(s // tm,),
            in_specs=[pl.BlockSpec((None, 1, tm * TOP_K), lambda i, *_: (i, 0, 0),
                                   memory_space=pltpu.SMEM),
                      pl.BlockSpec((tm, d), lambda i, *_: (i, 0))],
            out_specs=pl.BlockSpec(memory_space=pl.ANY),
            scratch_shapes=[pltpu.VMEM((MOE_TM // 2, d), u.dtype),
                            pltpu.SemaphoreType.DMA(()), pltpu.SemaphoreType.DMA(())]),
        out_shape=jax.ShapeDtypeStruct((n_tiles * MOE_TM, d), u.dtype),
        compiler_params=_params(("arbitrary",)),
        name="moe_dispatch",
    )(pad_row, pad_len, n_used, dest3, u)


def _experts_kernel(te_ref, nu_ref, x_ref, wgu_ref, bgu_ref, wd_ref, y_ref, wgu16_ref, wd16_ref):
    i = pl.program_id(0)
    new_expert = jnp.logical_or(i == 0, te_ref[i] != te_ref[jnp.maximum(i - 1, 0)])

    @pl.when(new_expert)
    def _():
        for r0 in range(0, wgu_ref.shape[0], W_SLAB):
            rs = slice(r0, r0 + W_SLAB)
            wgu16_ref[rs, :] = wgu_ref[rs, :].astype(bf16)
        wd16_ref[...] = wd_ref[...].astype(bf16)

    @pl.when(i < nu_ref[0])
    def _():
        lo, hi = _unpack_halves(x_ref[...])
        half = lo.shape[1]
        gu = _dot(lo, wgu16_ref[:half, :]) + _dot(hi, wgu16_ref[half:, :]) + bgu_ref[...]
        gate = jnp.minimum(gu[:, :D_EXPERT], SWIGLU_LIMIT)
        up = jnp.clip(gu[:, D_EXPERT:], -SWIGLU_LIMIT, SWIGLU_LIMIT)
        h = (up + 1.0) * gate * jax.nn.sigmoid(SWIGLU_ALPHA * gate)
        y_ref[...] = _pack_halves(_dot(h.astype(bf16), wd16_ref[...]))

    @pl.when(i >= nu_ref[0])
    def _():
        y_ref[...] = jnp.zeros_like(y_ref)


def _experts(us, tile_expert, n_used, w_gate_up, b_gate_up, w_down, layer):
    rows = us.shape[0]
    d = w_down.shape[-1]
    return pl.pallas_call(
        _experts_kernel,
        grid_spec=pltpu.PrefetchScalarGridSpec(
            num_scalar_prefetch=2,
            grid=(rows // MOE_TM,),
            in_specs=[pl.BlockSpec((MOE_TM, d // 2),
                                   lambda i, te, nu: (jnp.minimum(i, nu[0] - 1), 0)),
                      pl.BlockSpec((None, None, d, 2 * D_EXPERT),
                                   lambda i, te, nu: (layer, te[i], 0, 0)),
                      pl.BlockSpec((None, None, 1, 2 * D_EXPERT),
                                   lambda i, te, nu: (layer, te[i], 0, 0)),
                      pl.BlockSpec((None, None, D_EXPERT, d),
                                   lambda i, te, nu: (layer, te[i], 0, 0))],
            out_specs=pl.BlockSpec((MOE_TM, d // 2), lambda i, te, nu: (i, 0)),
            scratch_shapes=[pltpu.VMEM((d, 2 * D_EXPERT), bf16),
                            pltpu.VMEM((D_EXPERT, d), bf16)]),
        out_shape=jax.ShapeDtypeStruct((rows, d // 2), jnp.uint32),
        compiler_params=_params(("arbitrary",)),
        name="moe_experts",
    )(tile_expert, n_used, us, w_gate_up, b_gate_up, w_down)


def _combine_ln_kernel(dest_ref, dnext_ref, x_ref, w_ref, comb_ref, bd_ref, gt_ref, g_ref, b_ref,
                       sc_ref, sh_ref, ys_ref, xo_ref, uo_ref, buf_ref, sem):
    i = pl.program_id(0)
    n = pl.num_programs(0)
    tm = x_ref.shape[0]
    slot = i % 2

    def gather(d_ref, to):
        for r in range(tm):
            for k in range(TOP_K):
                row = d_ref[0, r * TOP_K + k]
                pltpu.make_async_copy(ys_ref.at[pl.ds(row, 1)],
                                      buf_ref.at[to, k, pl.ds(r, 1)], sem.at[to]).start()

    def wait(at):
        for k in range(TOP_K):
            pltpu.make_async_copy(ys_ref.at[pl.ds(0, tm)], buf_ref.at[at, k], sem.at[at]).wait()

    @pl.when(i == 0)
    def _():
        gather(dest_ref, 0)

    wait(slot)
    gather(dnext_ref, 1 - slot)

    w_tile = w_ref[...]
    lane = lax.broadcasted_iota(jnp.int32, w_tile.shape, 1)
    y_lo = y_hi = None
    for k in range(TOP_K):
        w_k = jnp.sum(jnp.where(lane == k, w_tile, 0.0), axis=1, keepdims=True)
        lo, hi = _unpack_halves_f32(buf_ref[slot, k])
        y_lo = w_k * lo if k == 0 else y_lo + w_k * lo
        y_hi = w_k * hi if k == 0 else y_hi + w_k * hi
    y = _dot3(comb_ref[...], bd_ref[...]) + jnp.concatenate([y_lo, y_hi], axis=1)
    z = DN_ALPHA * x_ref[...] + gt_ref[...] * y
    xn = _layernorm(z, g_ref[...], b_ref[...])
    xo_ref[...] = xn
    uo_ref[...] = (xn * (1.0 + sc_ref[...]) + sh_ref[...]).astype(uo_ref.dtype)

    @pl.when(i == n - 1)
    def _():
        wait(1 - slot)


def _combine_ln(x, ys, dest, w_tile, comb, b_down, gt, g, b, sc, sh, layer):
    s, d = x.shape
    tm = min(COMB_TM, s)
    steps = s // tm
    dest3 = dest.reshape(steps, 1, tm * TOP_K)
    tile = pl.BlockSpec((tm, d), lambda i: (i, 0))
    row = pl.BlockSpec((1, d), lambda i: (0, 0))
    lrow = pl.BlockSpec((None, 1, d), lambda i: (layer, 0, 0))
    return pl.pallas_call(
        _combine_ln_kernel,
        grid=(steps,),
        in_specs=[pl.BlockSpec((None, 1, tm * TOP_K), lambda i: (i, 0, 0), memory_space=pltpu.SMEM),
                  pl.BlockSpec((None, 1, tm * TOP_K),
                               lambda i: (jnp.minimum(i + 1, steps - 1), 0, 0),
                               memory_space=pltpu.SMEM),
                  tile,
                  pl.BlockSpec((tm, LANES), lambda i: (i, 0)),
                  pl.BlockSpec((tm, N_EXPERTS), lambda i: (i, 0)),
                  pl.BlockSpec((None, N_EXPERTS, d), lambda i: (layer, 0, 0)),
                  row, lrow, lrow, row, row,
                  pl.BlockSpec(memory_space=pl.ANY)],
        out_specs=[tile, tile],
        out_shape=[jax.ShapeDtypeStruct((s, d), f32), jax.ShapeDtypeStruct((s, d), bf16)],
        scratch_shapes=[pltpu.VMEM((2, TOP_K, tm, d // 2), jnp.uint32),
                        pltpu.SemaphoreType.DMA((2,))],
        compiler_params=_params(("arbitrary",)),
        name="moe_combine_ln",
    )(dest3, dest3, x, w_tile, comb, b_down, gt, g, b, sc, sh, ys)


def _silu(x):
    return x * jax.nn.sigmoid(x)


def _l2norm(x):
    return x * lax.rsqrt(jnp.sum(x * x, axis=-1, keepdims=True) + NORM_EPS)


def _ba_kernel(u_ref, w_ref, alog_ref, dt_ref, o_ref):
    tm = u_ref.shape[0]
    ba = _dot_nt(u_ref[...], w_ref[...].astype(bf16))
    o_ref[:, :LANES] = jax.nn.sigmoid(ba)
    xa = ba + dt_ref[...]
    softplus = jnp.maximum(xa, 0.0) + jnp.log(1.0 + jnp.exp(-jnp.abs(xa)))
    g = -jnp.exp(alog_ref[...]) * softplus
    row_i = lax.broadcasted_iota(jnp.int32, (CHUNK, CHUNK), 0)
    col_i = lax.broadcasted_iota(jnp.int32, (CHUNK, CHUNK), 1)
    tril = jnp.where(row_i >= col_i, 1.0, 0.0).astype(bf16)
    for c in range(tm // CHUNK):
        r = slice(c * CHUNK, (c + 1) * CHUNK)
        g1 = g[r].astype(bf16)
        r1 = g[r] - g1.astype(f32)
        g2 = r1.astype(bf16)
        g3 = (r1 - g2.astype(f32)).astype(bf16)
        o_ref[r, LANES:] = _dot(tril, g1) + _dot(tril, g2) + _dot(tril, g3)


def _ba(u, w_in_t, alog, dtb, layer):
    s, d = u.shape
    tm = min(1024, s)
    lrow = pl.BlockSpec((None, 1, LANES), lambda i: (layer, 0, 0))
    return pl.pallas_call(
        _ba_kernel,
        grid=(s // tm,),
        in_specs=[pl.BlockSpec((tm, d), lambda i: (i, 0)),
                  pl.BlockSpec((None, LANES, d), lambda i: (layer, OFF_BA // LANES, 0)),
                  lrow, lrow],
        out_specs=pl.BlockSpec((tm, 2 * LANES), lambda i: (i, 0)),
        out_shape=jax.ShapeDtypeStruct((s, 2 * LANES), f32),
        compiler_params=_params(("parallel",)),
        name="beta_decay",
    )(u, w_in_t, alog, dtb)


def _delta_kernel(heads,
                  q_ref, k_ref, v_ref, z_ref, qh_ref, kh_ref, vh_ref, beta_ref, cum_ref,
                  cq_ref, ck_ref, cv_ref, na_ref,
                  o_ref, state_ref, xs_ref, cumt_ref):
    hp = pl.program_id(0)
    i = pl.program_id(1)

    @pl.when(i == 0)
    def _():
        state_ref[...] = jnp.zeros_like(state_ref)

    first = i == 0
    row_i = lax.broadcasted_iota(jnp.int32, (CHUNK, CHUNK), 0)
    col_i = lax.broadcasted_iota(jnp.int32, (CHUNK, CHUNK), 1)
    causal = row_i >= col_i
    strict = row_i > col_i
    eye = jnp.where(row_i == col_i, 1.0, 0.0)
    merge_masks = []
    for shift in range(int(math.log2(CHUNK))):
        rb = row_i >> shift
        merge_masks.append(jnp.logical_and((rb & 1) == 1, (col_i >> shift) == rb - 1))

    def conv_silu(x_ref, halo_ref, cw_ref):
        xs_ref[0:8, :] = jnp.where(first, 0.0, halo_ref[...])
        xs_ref[8:8 + CHUNK, :] = x_ref[...]
        acc = cw_ref[CONV_K - 1:CONV_K, :] * xs_ref[8:8 + CHUNK, :]
        for back in range(1, CONV_K):
            tap = CONV_K - 1 - back
            acc = acc + cw_ref[tap:tap + 1, :] * xs_ref[8 - back:8 - back + CHUNK, :]
        return _silu(acc)

    q_all = conv_silu(q_ref, qh_ref, cq_ref)
    k_all = conv_silu(k_ref, kh_ref, ck_ref)
    v_all = conv_silu(v_ref, vh_ref, cv_ref)
    beta_t = beta_ref[...]
    cum_tile = cum_ref[...]
    cumt_ref[...] = cum_tile.T

    hs = range(heads)
    sls = [slice(j * LANES, (j + 1) * LANES) for j in hs]
    q, k, v, beta_b, cum, decay = [], [], [], [], [], []
    for j in hs:
        head = hp * heads + j
        q.append(_l2norm(q_all[:, sls[j]]) * (DKA ** -0.5))
        k.append(_l2norm(k_all[:, sls[j]]))
        v.append(v_all[:, sls[j]])
        beta_b.append(jnp.broadcast_to(
            jnp.sum(jnp.where(col_i == head, beta_t, 0.0), axis=1, keepdims=True), (CHUNK, CHUNK)))
        cum.append(jnp.broadcast_to(
            jnp.sum(jnp.where(col_i == head + HA, cum_tile, 0.0), axis=1, keepdims=True),
            (CHUNK, CHUNK)))
        cum_t = jnp.broadcast_to(cumt_ref[pl.ds(head + HA, 1), :], (CHUNK, CHUNK))
        decay.append(jnp.exp(jnp.where(causal, cum[j] - cum_t, NEG_BIG)))
    k16 = [k[j].astype(bf16) for j in hs]
    kb = [k[j] * beta_b[j] for j in hs]
    lmat = [jnp.where(strict, _dot_nt(kb[j].astype(bf16), k16[j]) * decay[j], 0.0) for j in hs]
    intra = [jnp.where(causal, _dot_nt(q[j].astype(bf16), k16[j]) * decay[j], 0.0).astype(bf16)
             for j in hs]
    tn = [jnp.where(merge_masks[0], -lmat[j], 0.0) for j in hs]
    for mask in merge_masks[1:]:
        t_inv = [(eye + tn[j]).astype(bf16) for j in hs]
        half = [_dot(t_inv[j], jnp.where(mask, lmat[j], 0.0).astype(bf16)).astype(bf16) for j in hs]
        tn = [tn[j] - _dot(half[j], t_inv[j]) for j in hs]
    e_g = [jnp.exp(cum[j]) for j in hs]
    g_last = [cum[j][CHUNK - 1:CHUNK, :] for j in hs]
    rhs = [jnp.concatenate([v[j] * beta_b[j], kb[j] * e_g[j]], axis=1) for j in hs]
    uw = [rhs[j] + _dot(tn[j].astype(bf16), rhs[j].astype(bf16)) for j in hs]
    k_dec_t = [(k[j] * jnp.exp(g_last[j] - cum[j])).T.astype(bf16) for j in hs]
    state = [state_ref[j] for j in hs]
    ws = [_dot(jnp.concatenate([uw[j][:, DVA:], q[j] * e_g[j]], axis=0).astype(bf16),
               state[j].astype(bf16)) for j in hs]
    v16 = [(uw[j][:, :DVA] - ws[j][:CHUNK]).astype(bf16) for j in hs]
    o_c = [ws[j][CHUNK:] + _dot(intra[j], v16[j]) for j in hs]
    for j in hs:
        state_ref[j] = state[j] * jnp.exp(g_last[j]) + _dot(k_dec_t[j], v16[j])
    for j in hs:
        o_n = o_c[j] * lax.rsqrt(jnp.mean(o_c[j] * o_c[j], axis=-1, keepdims=True) + NORM_EPS)
        o_ref[:, sls[j]] = (o_n * na_ref[...] * _silu(z_ref[:, sls[j]])).astype(o_ref.dtype)


def _delta(proj_a, bg, conv_w, norm_a, layer):
    s = proj_a.shape[0]
    heads = 16
    hw = heads * LANES
    halo_blocks = CHUNK // 8

    def tile(col0):
        return pl.BlockSpec((CHUNK, hw), lambda hp, i: (i, hp + col0 // hw))

    def halo(col0):
        return pl.BlockSpec(
            (8, hw), lambda hp, i: (jnp.maximum(i * halo_blocks - 1, 0), hp + col0 // hw))

    def cw(col0):
        return pl.BlockSpec((None, CONV_K, hw), lambda hp, i: (layer, 0, hp + col0 // hw))

    return pl.pallas_call(
        functools.partial(_delta_kernel, heads),
        grid=(WA // hw, s // CHUNK),
        in_specs=[tile(0), tile(WA), tile(2 * WA), tile(2 * WA + WVA),
                  halo(0), halo(WA), halo(2 * WA),
                  pl.BlockSpec((CHUNK, LANES), lambda hp, i: (i, 0)),
                  pl.BlockSpec((CHUNK, LANES), lambda hp, i: (i, 1)),
                  cw(0), cw(WA), cw(2 * WA),
                  pl.BlockSpec((None, 1, LANES), lambda hp, i: (layer, 0, 0))],
        out_specs=pl.BlockSpec((CHUNK, hw), lambda hp, i: (i, hp)),
        out_shape=jax.ShapeDtypeStruct((s, WVA), bf16),
        scratch_shapes=[pltpu.VMEM((heads, DKA, DVA), f32),
                        pltpu.VMEM((CHUNK + 8, hw), f32),
                        pltpu.VMEM((LANES, CHUNK), f32)],
        compiler_params=_params(("parallel", "arbitrary")),
        name="gated_delta",
    )(proj_a, proj_a, proj_a, proj_a, proj_a, proj_a, proj_a, bg, bg,
      conv_w, conv_w, conv_w, norm_a)


def _attn_kernel(layer, q_ref, kp_ref, kc_ref, vp_ref, vc_ref, bm_ref, sink_ref, o_ref):
    n = pl.program_id(0)
    lane = lax.broadcasted_iota(jnp.int32, (2 * BLK, LANES), 1)
    low = lane < DHB
    key_j = lax.broadcasted_iota(jnp.int32, (BLK, 2 * BLK), 1)
    no_prev = jnp.logical_and(n == 0, key_j < BLK)
    scale = DHB ** -0.5
    groups_per_tile = LANES // DHB
    pairs_per_group = HB // HB_KV // 2
    k_cat, v_cat = [], []
    for t in range(WKVB // LANES):
        sl = slice(t * LANES, (t + 1) * LANES)
        k_t = jnp.concatenate([kp_ref[:, sl], kc_ref[:, sl]], axis=0)
        v_t = jnp.concatenate([vp_ref[:, sl], vc_ref[:, sl]], axis=0)
        for gg in range(groups_per_tile):
            mine = low if gg == 0 else jnp.logical_not(low)
            k_m = jnp.where(mine, k_t, 0.0)
            v_m = jnp.where(mine, v_t, 0.0)
            k_2 = k_m + pltpu.roll(k_m, DHB, axis=1)
            v_2 = v_m + pltpu.roll(v_m, DHB, axis=1)
            k_cat.append(jnp.concatenate([jnp.where(low, k_2, 0.0),
                                          jnp.where(low, 0.0, k_2)], axis=0).astype(bf16))
            v_cat.append(jnp.concatenate([jnp.where(low, v_2, 0.0),
                                          jnp.where(low, 0.0, v_2)], axis=0).astype(bf16))
    pairs = range(HB // 2)
    qls = [slice(p * LANES, (p + 1) * LANES) for p in pairs]
    scores = [_dot_nt((q_ref[:, qls[p]] * scale).astype(bf16), k_cat[p // pairs_per_group])
              for p in pairs]
    probs = []
    for p in pairs:
        halves = []
        for hh in range(2):
            head = 2 * p + hh
            sh = scores[p][:, hh * 2 * BLK:(hh + 1) * 2 * BLK] + bm_ref[head]
            sh = jnp.where(no_prev, NEG_BIG, sh)
            sink = sink_ref[layer, head]
            mx = jnp.maximum(jnp.max(sh, axis=-1, keepdims=True), sink)
            e = jnp.exp(sh - mx)
            den = jnp.sum(e, axis=-1, keepdims=True) + jnp.exp(sink - mx)
            halves.append((e * (1.0 / den)).astype(bf16))
        probs.append(jnp.concatenate(halves, axis=1))
    for p in pairs:
        o_ref[:, qls[p]] = _dot(probs[p], v_cat[p // pairs_per_group]).astype(o_ref.dtype)


def _attn(proj_b, bias_mask, sinks, layer):
    s = proj_b.shape[0]
    kb0 = WQB // WKVB
    return pl.pallas_call(
        functools.partial(_attn_kernel, layer),
        grid=(s // BLK,),
        in_specs=[pl.BlockSpec((BLK, WQB), lambda n: (n, 0)),
                  pl.BlockSpec((BLK, WKVB), lambda n: (jnp.maximum(n - 1, 0), kb0)),
                  pl.BlockSpec((BLK, WKVB), lambda n: (n, kb0)),
                  pl.BlockSpec((BLK, WKVB), lambda n: (jnp.maximum(n - 1, 0), kb0 + 1)),
                  pl.BlockSpec((BLK, WKVB), lambda n: (n, kb0 + 1)),
                  pl.BlockSpec((HB, BLK, 2 * BLK), lambda n: (0, 0, 0)),
                  pl.BlockSpec(memory_space=pltpu.SMEM)],
        out_specs=pl.BlockSpec((BLK, WQB), lambda n: (n, 0)),
        out_shape=jax.ShapeDtypeStruct((s, WQB), bf16),
        compiler_params=_params(("parallel",)),
        name="swa_attention",
    )(proj_b, proj_b, proj_b, proj_b, proj_b, bias_mask, sinks)


def _t5_bucket(d):
    n_exact = N_BUCKETS // 2
    df = jnp.maximum(d, 1).astype(f32)
    large = n_exact + (jnp.log(df / n_exact) / math.log(MAX_DIST / n_exact)
                       * (N_BUCKETS - n_exact)).astype(jnp.int32)
    large = jnp.minimum(large, N_BUCKETS - 1)
    return jnp.where(d < n_exact, d, large)


def _bias_kernel(rbt_ref, bucket_ref, inside_ref, o_ref):
    n = bucket_ref.shape[1]
    rows = lax.broadcasted_iota(jnp.int32, (N_BUCKETS, n), 0)
    onehot = jnp.where(rows == bucket_ref[...], 1.0, 0.0).astype(bf16)
    r = rbt_ref[...]
    r1 = r.astype(bf16)
    rem = r - r1.astype(f32)
    r2 = rem.astype(bf16)
    r3 = (rem - r2.astype(f32)).astype(bf16)
    table = _dot(r1, onehot) + _dot(r2, onehot) + _dot(r3, onehot)
    o_ref[...] = jnp.where(inside_ref[...] != 0, table, NEG_BIG)


def _band_bias(rel_bias):
    i = jnp.arange(BLK)[:, None]
    j = jnp.arange(2 * BLK)[None, :]
    d = i + BLK - j
    inside = ((d >= 0) & (d < WINDOW)).astype(jnp.int32).reshape(1, -1)
    bucket = _t5_bucket(jnp.clip(d, 0, MAX_DIST - 1)).astype(jnp.int32).reshape(1, -1)
    n = bucket.shape[1]
    tn = 4096
    flat = pl.pallas_call(
        _bias_kernel,
        grid=(n // tn,),
        in_specs=[pl.BlockSpec((HB, N_BUCKETS), lambda c: (0, 0)),
                  pl.BlockSpec((1, tn), lambda c: (0, c)),
                  pl.BlockSpec((1, tn), lambda c: (0, c))],
        out_specs=pl.BlockSpec((HB, tn), lambda c: (0, c)),
        out_shape=jax.ShapeDtypeStruct((HB, n), f32),
        compiler_params=_params(("parallel",)),
        name="band_bias",
    )(rel_bias.astype(f32).T, bucket, inside)
    return flat.reshape(HB, BLK, 2 * BLK)


def kernel(x, c, w_ada, b_ada, ada_table, rel_bias, w_in, conv_w, a_log, dt_bias, norm_a, sinks,
           w_up_a, w_up_b, w_o, ln1_g, ln1_b, w_router, b_router, w_gate_up, b_gate_up, w_down,
           b_down, ln2_g, ln2_b):
    bsz, s, d = x.shape
    assert bsz == 1 and d == D_MODEL and s % BLK == 0
    depth = w_in.shape[0]
    xs = x.reshape(s, d)

    mod_base = _ada(jnp.broadcast_to(c, (8, d)), w_ada, b_ada)[0]
    mod = mod_base.reshape(1, N_MOD, d) + ada_table

    b_gu = b_gate_up.reshape(depth, N_EXPERTS, 1, 2 * D_EXPERT)
    b_rt = b_router.reshape(depth, 1, N_EXPERTS)
    pad_h = ((0, 0), (HA, LANES - 2 * HA))
    alog = jnp.pad(a_log, pad_h).reshape(depth, 1, LANES)
    dtb = jnp.pad(dt_bias, pad_h).reshape(depth, 1, LANES)
    na = norm_a.reshape(depth, 1, DVA)
    g1 = ln1_g.reshape(depth, 1, d)
    b1 = ln1_b.reshape(depth, 1, d)
    g2 = ln2_g.reshape(depth, 1, d)
    b2 = ln2_b.reshape(depth, 1, d)
    bias_mask = _band_bias(rel_bias)
    w_in_t = jnp.swapaxes(w_in, 1, 2)

    def mrow(layer, idx):
        return mod[layer, idx].reshape(1, d)

    u = _modulate(xs, mrow(0, 1), mrow(0, 0))
    for layer in range(depth):
        proj_a = _matmul(u, w_in_t, layer, 0, W_PROJ_A, f32, 1024, 512, "in_proj_a", True)
        proj_b = _matmul(u, w_in_t, layer, OFF_B, W_PROJ_B, f32, 1024, 512, "in_proj_b", True)
        bg = _ba(u, w_in_t, alog, dtb, layer)
        o_a = _delta(proj_a, bg, conv_w, na, layer)
        o_b = _attn(proj_b, bias_mask, sinks, layer)
        merged = _merge(o_a, o_b, w_up_a, w_up_b, proj_b, layer)
        y = _matmul(merged, w_o, layer, 0, d, bf16, 1024, 512, "out_proj")
        xs, u_moe, comb, id_tile, w_tile = _ln_route(
            xs, y, mrow(layer, 2), g1, b1, mrow(layer, 4), mrow(layer, 3), w_router, b_rt, layer)
        dest, tile_expert, n_used, pad_row, pad_len, n_tiles = _plan(id_tile, s)
        us = _dispatch(u_moe, dest, pad_row, pad_len, n_used, n_tiles)
        ys = _experts(us, tile_expert, n_used, w_gate_up, b_gu, w_down, layer)
        nxt = min(layer + 1, depth - 1)
        xs, u = _combine_ln(xs, ys, dest, w_tile, comb, b_down, mrow(layer, 5), g2, b2,
                            mrow(nxt, 1), mrow(nxt, 0), layer)
    return xs.reshape(bsz, s, d)
```

```python
import functools
import math

import jax
import jax.numpy as jnp
import numpy as np
from jax import lax
from jax.experimental import pallas as pl
from jax.experimental.pallas import tpu as pltpu

D_MODEL = 4096
DEPTH = 4
HA = 16
DKA = 128
DVA = 128
CONV_K = 4
HB = 32
HB_KV = 8
DHB = 64
WINDOW = 128
BLK = 128
N_BUCKETS = 32
MAX_DIST = WINDOW
N_EXPERTS = 32
TOP_K = 4
D_EXPERT = 256
SWIGLU_LIMIT = 7.0
SWIGLU_ALPHA = 1.702
DN_ALPHA = (2 * DEPTH) ** 0.25
LN_EPS = 1e-5
NORM_EPS = 1e-6
N_MOD = 6
WA = HA * DKA
WVA = HA * DVA
WQB = HB * DHB
WKVB = HB_KV * DHB
OFF_BA = 2 * WA + 2 * WVA
OFF_B = OFF_BA + 2 * HA
W_PROJ_A = OFF_BA
W_PROJ_B = WQB + 2 * WKVB + 2 * D_MODEL

LANES = 128
CHUNK = 128
VMEM_LIMIT = 56 * 1024 * 1024
NEG_BIG = -1e30

bf16 = jnp.bfloat16
f32 = jnp.float32


def _dot(a, b):
    return jnp.dot(a, b, preferred_element_type=f32)


def _dot_nt(a, b):
    return lax.dot_general(a, b, (((1,), (1,)), ((), ())), preferred_element_type=f32)


def _split2(x):
    hi = x.astype(bf16)
    lo = (x - hi.astype(f32)).astype(bf16)
    return hi, lo


def _dot3(a, b):
    ah, al = _split2(a)
    bh, bl = _split2(b)
    return _dot(ah, bh) + _dot(ah, bl) + _dot(al, bh)


def _params(sem):
    return pltpu.CompilerParams(dimension_semantics=sem, vmem_limit_bytes=VMEM_LIMIT)


def _ada_kernel(c_ref, w_ref, b_ref, o_ref):
    c = c_ref[...]
    a = c * jax.nn.sigmoid(c)
    o_ref[...] = _dot3(a, w_ref[...]) + b_ref[...]


def _ada(c8, w_ada, b_ada):
    n = w_ada.shape[1]
    tn = 512
    return pl.pallas_call(
        _ada_kernel,
        grid=(n // tn,),
        in_specs=[pl.BlockSpec((8, D_MODEL), lambda j: (0, 0)),
                  pl.BlockSpec((D_MODEL, tn), lambda j: (0, j)),
                  pl.BlockSpec((1, tn), lambda j: (0, j))],
        out_specs=pl.BlockSpec((8, tn), lambda j: (0, j)),
        out_shape=jax.ShapeDtypeStruct((8, n), f32),
        compiler_params=_params(("parallel",)),
        name="ada_mod",
    )(c8, w_ada, b_ada.reshape(1, n))


def _modulate_kernel(x_ref, sc_ref, sh_ref, u_ref):
    u_ref[...] = (x_ref[...] * (1.0 + sc_ref[...]) + sh_ref[...]).astype(u_ref.dtype)


def _modulate(x, sc, sh):
    s, d = x.shape
    tm = min(256, s)
    row = pl.BlockSpec((1, d), lambda i: (0, 0))
    return pl.pallas_call(
        _modulate_kernel,
        grid=(s // tm,),
        in_specs=[pl.BlockSpec((tm, d), lambda i: (i, 0)), row, row],
        out_specs=pl.BlockSpec((tm, d), lambda i: (i, 0)),
        out_shape=jax.ShapeDtypeStruct((s, d), bf16),
        compiler_params=_params(("parallel",)),
        name="modulate",
    )(x, sc, sh)


W_SLAB = 512


def _mm_kernel(transposed, a_ref, w_ref, o_ref, wb_ref):
    first = pl.program_id(1) == 0

    @pl.when(first)
    def _():
        if transposed:
            o_ref[...] = _dot_nt(a_ref[...], w_ref[0].astype(bf16)).astype(o_ref.dtype)
            for r0 in range(0, wb_ref.shape[0], W_SLAB):
                rs = slice(r0, r0 + W_SLAB)
                wb_ref[rs, :] = w_ref[0, :, rs].T.astype(bf16)
        else:
            w16 = w_ref[...].astype(bf16)
            o_ref[...] = _dot(a_ref[...], w16).astype(o_ref.dtype)
            wb_ref[...] = w16

    @pl.when(jnp.logical_not(first))
    def _():
        o_ref[...] = _dot(a_ref[...], wb_ref[...]).astype(o_ref.dtype)


def _matmul(a, w, layer, col0, ncols, out_dtype, tm, tn, name, transposed=False):
    m, k = a.shape
    tm = min(tm, m)
    assert ncols % tn == 0 and k % W_SLAB == 0
    if transposed:
        assert col0 % 8 == 0
        w_spec = pl.BlockSpec((pl.Element(1), pl.Element(tn), pl.Element(k)),
                              lambda j, i: (layer, pl.multiple_of(col0 + j * tn, 8), 0))
    else:
        assert col0 % tn == 0
        w_spec = pl.BlockSpec((None, k, tn), lambda j, i: (layer, 0, j + col0 // tn))
    return pl.pallas_call(
        functools.partial(_mm_kernel, transposed),
        grid=(ncols // tn, m // tm),
        in_specs=[pl.BlockSpec((tm, k), lambda j, i: (i, 0)), w_spec],
        out_specs=pl.BlockSpec((tm, tn), lambda j, i: (i, j)),
        out_shape=jax.ShapeDtypeStruct((m, ncols), out_dtype),
        scratch_shapes=[pltpu.VMEM((k, tn), bf16)],
        compiler_params=_params(("arbitrary", "arbitrary")),
        name=name,
    )(a, w)


def _merge_kernel(oa_ref, ob_ref, wa_ref, wb_ref, ga_ref, gb_ref, o_ref, wa16_ref, wb16_ref):
    @pl.when(pl.program_id(1) == 0)
    def _():
        wa16_ref[...] = wa_ref[...].astype(bf16)
        wb16_ref[...] = wb_ref[...].astype(bf16)

    ya = _dot(oa_ref[...], wa16_ref[...])
    yb = _dot(ob_ref[...], wb16_ref[...])
    o_ref[...] = (jax.nn.sigmoid(ga_ref[...]) * ya
                  + jax.nn.sigmoid(gb_ref[...]) * yb).astype(o_ref.dtype)


def _merge(oa, ob, w_up_a, w_up_b, proj_b, layer):
    s = oa.shape[0]
    tm = min(1024, s)
    tn = 512
    ga0 = (WQB + 2 * WKVB) // tn
    gb0 = ga0 + D_MODEL // tn
    return pl.pallas_call(
        _merge_kernel,
        grid=(D_MODEL // tn, s // tm),
        in_specs=[pl.BlockSpec((tm, WVA), lambda j, i: (i, 0)),
                  pl.BlockSpec((tm, WQB), lambda j, i: (i, 0)),
                  pl.BlockSpec((None, WVA, tn), lambda j, i: (layer, 0, j)),
                  pl.BlockSpec((None, WQB, tn), lambda j, i: (layer, 0, j)),
                  pl.BlockSpec((tm, tn), lambda j, i: (i, j + ga0)),
                  pl.BlockSpec((tm, tn), lambda j, i: (i, j + gb0))],
        out_specs=pl.BlockSpec((tm, tn), lambda j, i: (i, j)),
        out_shape=jax.ShapeDtypeStruct((s, D_MODEL), bf16),
        scratch_shapes=[pltpu.VMEM((WVA, tn), bf16), pltpu.VMEM((WQB, tn), bf16)],
        compiler_params=_params(("arbitrary", "arbitrary")),
        name="gated_merge",
    )(oa, ob, w_up_a, w_up_b, proj_b, proj_b)


def _layernorm(z, g, b):
    mu = jnp.mean(z, axis=-1, keepdims=True)
    zc = z - mu
    var = jnp.mean(zc * zc, axis=-1, keepdims=True)
    return zc * lax.rsqrt(var + LN_EPS) * g + b


def _route(logits):
    tm, n_e = logits.shape
    lane = lax.broadcasted_iota(jnp.int32, logits.shape, 1)
    slot = lax.broadcasted_iota(jnp.int32, (tm, LANES), 1)
    work = logits
    ids, vals = [], []
    for _ in range(TOP_K):
        m = jnp.max(work, axis=-1, keepdims=True)
        idx = jnp.min(jnp.where(work == m, lane, n_e), axis=-1, keepdims=True)
        work = jnp.where(lane == idx, -jnp.inf, work)
        ids.append(idx)
        vals.append(m)
    es = [jnp.exp(v - vals[0]) for v in vals]
    inv = 1.0 / functools.reduce(lambda p, q: p + q, es)
    comb = jnp.zeros(logits.shape, f32)
    id_tile = jnp.zeros((tm, LANES), jnp.int32)
    w_tile = jnp.zeros((tm, LANES), f32)
    for r in range(TOP_K):
        w = es[r] * inv
        comb = jnp.where(lane == ids[r], w, comb)
        id_tile = jnp.where(slot == r, ids[r], id_tile)
        w_tile = jnp.where(slot == r, w, w_tile)
    return comb, id_tile, w_tile


def _pack_halves(x):
    n = x.shape[1] // 2
    lo = pltpu.bitcast(x[:, :n].astype(bf16).astype(f32), jnp.uint32)
    hi = pltpu.bitcast(x[:, n:].astype(bf16).astype(f32), jnp.uint32)
    return (lo >> 16) | (hi & jnp.uint32(0xFFFF0000))


def _unpack_halves(w):
    lo = pltpu.bitcast(w << 16, f32).astype(bf16)
    hi = pltpu.bitcast(w & jnp.uint32(0xFFFF0000), f32).astype(bf16)
    return lo, hi


def _unpack_halves_f32(w):
    return pltpu.bitcast(w << 16, f32), pltpu.bitcast(w & jnp.uint32(0xFFFF0000), f32)


def _ln_route_kernel(x_ref, y_ref, gt_ref, g_ref, b_ref, sc_ref, sh_ref, wr_ref, br_ref,
                     xo_ref, uo_ref, comb_ref, id_ref, w_ref):
    z = DN_ALPHA * x_ref[...] + gt_ref[...] * y_ref[...].astype(f32)
    xn = _layernorm(z, g_ref[...], b_ref[...])
    xo_ref[...] = xn
    u = xn * (1.0 + sc_ref[...]) + sh_ref[...]
    uo_ref[...] = _pack_halves(u)
    logits = _dot3(u, wr_ref[...]) + br_ref[...]
    comb_ref[...], id_ref[...], w_ref[...] = _route(logits)


def _ln_route(x, y, gt, g, b, sc, sh, w_router, b_router, layer):
    s, d = x.shape
    tm = min(256, s)
    tile = pl.BlockSpec((tm, d), lambda i: (i, 0))
    row = pl.BlockSpec((1, d), lambda i: (0, 0))
    lrow = pl.BlockSpec((None, 1, d), lambda i: (layer, 0, 0))
    small = pl.BlockSpec((tm, LANES), lambda i: (i, 0))
    packed = pl.BlockSpec((tm, d // 2), lambda i: (i, 0))
    return pl.pallas_call(
        _ln_route_kernel,
        grid=(s // tm,),
        in_specs=[tile, tile, row, lrow, lrow, row, row,
                  pl.BlockSpec((None, d, N_EXPERTS), lambda i: (layer, 0, 0)),
                  pl.BlockSpec((None, 1, N_EXPERTS), lambda i: (layer, 0, 0))],
        out_specs=[tile, packed, pl.BlockSpec((tm, N_EXPERTS), lambda i: (i, 0)), small, small],
        out_shape=[jax.ShapeDtypeStruct((s, d), f32), jax.ShapeDtypeStruct((s, d // 2), jnp.uint32),
                   jax.ShapeDtypeStruct((s, N_EXPERTS), f32),
                   jax.ShapeDtypeStruct((s, LANES), jnp.int32),
                   jax.ShapeDtypeStruct((s, LANES), f32)],
        compiler_params=_params(("parallel",)),
        name="residual_ln_route",
    )(x, y, gt, g, b, sc, sh, w_router, b_router)


MOE_TM = 256
DISP_TM = 256
COMB_TM = 256
PAD_BITS = tuple(1 << p for p in reversed(range(int(math.log2(MOE_TM)))))


def _plan(id_tile, s):
    idx = id_tile[:, :TOP_K]
    experts = jnp.arange(N_EXPERTS, dtype=jnp.int32)
    onehot = idx[:, :, None] == experts[None, None, :]
    mask = jnp.any(onehot, axis=1).astype(jnp.int32)
    incl = jnp.cumsum(mask, axis=0)
    counts = incl[-1]
    padded = (counts + MOE_TM - 1) // MOE_TM * MOE_TM
    ends = jnp.cumsum(padded)
    starts = ends - padded
    dest_all = starts[None, :] + incl - mask
    dest = jnp.sum(jnp.where(onehot, dest_all[:, None, :], 0), axis=-1)
    n_tiles = (TOP_K * s) // MOE_TM + N_EXPERTS
    end_tiles = ends // MOE_TM
    n_used = end_tiles[-1]
    tiles = jnp.arange(n_tiles, dtype=jnp.int32)
    tile_expert = jnp.sum((jnp.minimum(tiles, n_used - 1)[:, None] >= end_tiles[None, :])
                          .astype(jnp.int32), axis=1)
    tile_expert = jnp.minimum(tile_expert, N_EXPERTS - 1)
    return (dest.astype(jnp.int32), tile_expert.astype(jnp.int32),
            n_used.reshape(1).astype(jnp.int32), (starts + counts).astype(jnp.int32),
            (padded - counts).astype(jnp.int32), n_tiles)


def _dispatch_kernel(pad_row_ref, pad_len_ref, used_ref, dest_ref, u_ref, us_ref, zero_ref, sem, zsem):
    i = pl.program_id(0)
    tm = u_ref.shape[0]

    for r in range(tm):
        for k in range(TOP_K):
            row = dest_ref[0, r * TOP_K + k]
            pltpu.make_async_copy(u_ref.at[pl.ds(r, 1)], us_ref.at[pl.ds(row, 1)], sem).start()

    @pl.when(i == pl.num_programs(0) - 1)
    def _():
        zero_ref[...] = jnp.zeros_like(zero_ref)

        def fill(wait, e, carry):
            row = pad_row_ref[e]
            n = pad_len_ref[e]
            for bit in reversed(PAD_BITS):
                size = bit if bit >= 8 else 1
                for part in range(bit // size):
                    @pl.when((n & bit) != 0)
                    def _():
                        at = row + part * size
                        if size >= 8:
                            at = pl.multiple_of(at, 8)
                        cp = pltpu.make_async_copy(zero_ref.at[pl.ds(0, size)],
                                                   us_ref.at[pl.ds(at, size)], zsem)
                        if wait:
                            cp.wait()
                        else:
                            cp.start()
                row = row + (n & bit)
            return carry

        def tail(wait, t, carry):
            tile = used_ref[0] + t

            @pl.when(tile < us_ref.shape[0] // MOE_TM)
            def _():
                for part in range(MOE_TM // zero_ref.shape[0]):
                    at = pl.multiple_of(tile * MOE_TM + part * zero_ref.shape[0], 8)
                    cp = pltpu.make_async_copy(
                        zero_ref, us_ref.at[pl.ds(at, zero_ref.shape[0])], zsem)
                    if wait:
                        cp.wait()
                    else:
                        cp.start()
            return carry

        for wait in (False, True):
            lax.fori_loop(0, N_EXPERTS, functools.partial(fill, wait), 0)
            lax.fori_loop(0, N_EXPERTS, functools.partial(tail, wait), 0)

    for _ in range(TOP_K):
        pltpu.make_async_copy(u_ref, us_ref.at[pl.ds(0, tm)], sem).wait()


def _dispatch(u, dest, pad_row, pad_len, n_used, n_tiles):
    s, d = u.shape
    tm = min(DISP_TM, s)
    dest3 = dest.reshape(s // tm, 1, tm * TOP_K)
    return pl.pallas_call(
        _dispatch_kernel,
        grid_spec=pltpu.PrefetchScalarGridSpec(
            num_scalar_prefetch=3,
            grid=(s // tm,),
            in_specs=[pl.BlockSpec((None, 1, tm * TOP_K), lambda i, *_: (i, 0, 0),
                                   memory_space=pltpu.SMEM),
                      pl.BlockSpec((tm, d), lambda i, *_: (i, 0))],
            out_specs=pl.BlockSpec(memory_space=pl.ANY),
            scratch_shapes=[pltpu.VMEM((MOE_TM // 2, d), u.dtype),
                            pltpu.SemaphoreType.DMA(()), pltpu.SemaphoreType.DMA(())]),
        out_shape=jax.ShapeDtypeStruct((n_tiles * MOE_TM, d), u.dtype),
        compiler_params=_params(("arbitrary",)),
        name="moe_dispatch",
    )(pad_row, pad_len, n_used, dest3, u)


def _experts_kernel(te_ref, nu_ref, x_ref, wgu_ref, bgu_ref, wd_ref, y_ref, wgu16_ref, wd16_ref):
    i = pl.program_id(0)
    new_expert = jnp.logical_or(i == 0, te_ref[i] != te_ref[jnp.maximum(i - 1, 0)])

    @pl.when(new_expert)
    def _():
        for r0 in range(0, wgu_ref.shape[0], W_SLAB):
            rs = slice(r0, r0 + W_SLAB)
            wgu16_ref[rs, :] = wgu_ref[rs, :].astype(bf16)
        wd16_ref[...] = wd_ref[...].astype(bf16)

    @pl.when(i < nu_ref[0])
    def _():
        lo, hi = _unpack_halves(x_ref[...])
        half = lo.shape[1]
        gu = _dot(lo, wgu16_ref[:half, :]) + _dot(hi, wgu16_ref[half:, :]) + bgu_ref[...]
        gate = jnp.minimum(gu[:, :D_EXPERT], SWIGLU_LIMIT)
        up = jnp.clip(gu[:, D_EXPERT:], -SWIGLU_LIMIT, SWIGLU_LIMIT)
        h = (up + 1.0) * gate * jax.nn.sigmoid(SWIGLU_ALPHA * gate)
        y_ref[...] = _pack_halves(_dot(h.astype(bf16), wd16_ref[...]))

    @pl.when(i >= nu_ref[0])
    def _():
        y_ref[...] = jnp.zeros_like(y_ref)


def _experts(us, tile_expert, n_used, w_gate_up, b_gate_up, w_down, layer):
    rows = us.shape[0]
    d = w_down.shape[-1]
    return pl.pallas_call(
        _experts_kernel,
        grid_spec=pltpu.PrefetchScalarGridSpec(
            num_scalar_prefetch=2,
            grid=(rows // MOE_TM,),
            in_specs=[pl.BlockSpec((MOE_TM, d // 2),
                                   lambda i, te, nu: (jnp.minimum(i, nu[0] - 1), 0)),
                      pl.BlockSpec((None, None, d, 2 * D_EXPERT),
                                   lambda i, te, nu: (layer, te[i], 0, 0)),
                      pl.BlockSpec((None, None, 1, 2 * D_EXPERT),
                                   lambda i, te, nu: (layer, te[i], 0, 0)),
                      pl.BlockSpec((None, None, D_EXPERT, d),
                                   lambda i, te, nu: (layer, te[i], 0, 0))],
            out_specs=pl.BlockSpec((MOE_TM, d // 2), lambda i, te, nu: (i, 0)),
            scratch_shapes=[pltpu.VMEM((d, 2 * D_EXPERT), bf16),
                            pltpu.VMEM((D_EXPERT, d), bf16)]),
        out_shape=jax.ShapeDtypeStruct((rows, d // 2), jnp.uint32),
        compiler_params=_params(("arbitrary",)),
        name="moe_experts",
    )(tile_expert, n_used, us, w_gate_up, b_gate_up, w_down)


def _combine_ln_kernel(dest_ref, dnext_ref, x_ref, w_ref, comb_ref, bd_ref, gt_ref, g_ref, b_ref,
                       sc_ref, sh_ref, ys_ref, xo_ref, uo_ref, buf_ref, sem):
    i = pl.program_id(0)
    n = pl.num_programs(0)
    tm = x_ref.shape[0]
    slot = i % 2

    def gather(d_ref, to):
        for r in range(tm):
            for k in range(TOP_K):
                row = d_ref[0, r * TOP_K + k]
                pltpu.make_async_copy(ys_ref.at[pl.ds(row, 1)],
                                      buf_ref.at[to, k, pl.ds(r, 1)], sem.at[to]).start()

    def wait(at):
        for k in range(TOP_K):
            pltpu.make_async_copy(ys_ref.at[pl.ds(0, tm)], buf_ref.at[at, k], sem.at[at]).wait()

    @pl.when(i == 0)
    def _():
        gather(dest_ref, 0)

    wait(slot)
    gather(dnext_ref, 1 - slot)

    w_tile = w_ref[...]
    lane = lax.broadcasted_iota(jnp.int32, w_tile.shape, 1)
    y_lo = y_hi = None
    for k in range(TOP_K):
        w_k = jnp.sum(jnp.where(lane == k, w_tile, 0.0), axis=1, keepdims=True)
        lo, hi = _unpack_halves_f32(buf_ref[slot, k])
        y_lo = w_k * lo if k == 0 else y_lo + w_k * lo
        y_hi = w_k * hi if k == 0 else y_hi + w_k * hi
    y = _dot3(comb_ref[...], bd_ref[...]) + jnp.concatenate([y_lo, y_hi], axis=1)
    z = DN_ALPHA * x_ref[...] + gt_ref[...] * y
    xn = _layernorm(z, g_ref[...], b_ref[...])
    xo_ref[...] = xn
    uo_ref[...] = (xn * (1.0 + sc_ref[...]) + sh_ref[...]).astype(uo_ref.dtype)

    @pl.when(i == n - 1)
    def _():
        wait(1 - slot)


def _combine_ln(x, ys, dest, w_tile, comb, b_down, gt, g, b, sc, sh, layer):
    s, d = x.shape
    tm = min(COMB_TM, s)
    steps = s // tm
    dest3 = dest.reshape(steps, 1, tm * TOP_K)
    tile = pl.BlockSpec((tm, d), lambda i: (i, 0))
    row = pl.BlockSpec((1, d), lambda i: (0, 0))
    lrow = pl.BlockSpec((None, 1, d), lambda i: (layer, 0, 0))
    return pl.pallas_call(
        _combine_ln_kernel,
        grid=(steps,),
        in_specs=[pl.BlockSpec((None, 1, tm * TOP_K), lambda i: (i, 0, 0), memory_space=pltpu.SMEM),
                  pl.BlockSpec((None, 1, tm * TOP_K),
                               lambda i: (jnp.minimum(i + 1, steps - 1), 0, 0),
                               memory_space=pltpu.SMEM),
                  tile,
                  pl.BlockSpec((tm, LANES), lambda i: (i, 0)),
                  pl.BlockSpec((tm, N_EXPERTS), lambda i: (i, 0)),
                  pl.BlockSpec((None, N_EXPERTS, d), lambda i: (layer, 0, 0)),
                  row, lrow, lrow, row, row,
                  pl.BlockSpec(memory_space=pl.ANY)],
        out_specs=[tile, tile],
        out_shape=[jax.ShapeDtypeStruct((s, d), f32), jax.ShapeDtypeStruct((s, d), bf16)],
        scratch_shapes=[pltpu.VMEM((2, TOP_K, tm, d // 2), jnp.uint32),
                        pltpu.SemaphoreType.DMA((2,))],
        compiler_params=_params(("arbitrary",)),
        name="moe_combine_ln",
    )(dest3, dest3, x, w_tile, comb, b_down, gt, g, b, sc, sh, ys)


def _silu(x):
    return x * jax.nn.sigmoid(x)


def _l2norm(x):
    return x * lax.rsqrt(jnp.sum(x * x, axis=-1, keepdims=True) + NORM_EPS)


def _ba_kernel(u_ref, w_ref, alog_ref, dt_ref, o_ref):
    tm = u_ref.shape[0]
    ba = _dot_nt(u_ref[...], w_ref[...].astype(bf16))
    o_ref[:, :LANES] = jax.nn.sigmoid(ba)
    xa = ba + dt_ref[...]
    softplus = jnp.maximum(xa, 0.0) + jnp.log(1.0 + jnp.exp(-jnp.abs(xa)))
    g = -jnp.exp(alog_ref[...]) * softplus
    row_i = lax.broadcasted_iota(jnp.int32, (CHUNK, CHUNK), 0)
    col_i = lax.broadcasted_iota(jnp.int32, (CHUNK, CHUNK), 1)
    tril = jnp.where(row_i >= col_i, 1.0, 0.0).astype(bf16)
    for c in range(tm // CHUNK):
        r = slice(c * CHUNK, (c + 1) * CHUNK)
        g1 = g[r].astype(bf16)
        r1 = g[r] - g1.astype(f32)
        g2 = r1.astype(bf16)
        g3 = (r1 - g2.astype(f32)).astype(bf16)
        o_ref[r, LANES:] = _dot(tril, g1) + _dot(tril, g2) + _dot(tril, g3)


def _ba(u, w_in_t, alog, dtb, layer):
    s, d = u.shape
    tm = min(1024, s)
    lrow = pl.BlockSpec((None, 1, LANES), lambda i: (layer, 0, 0))
    return pl.pallas_call(
        _ba_kernel,
        grid=(s // tm,),
        in_specs=[pl.BlockSpec((tm, d), lambda i: (i, 0)),
                  pl.BlockSpec((None, LANES, d), lambda i: (layer, OFF_BA // LANES, 0)),
                  lrow, lrow],
        out_specs=pl.BlockSpec((tm, 2 * LANES), lambda i: (i, 0)),
        out_shape=jax.ShapeDtypeStruct((s, 2 * LANES), f32),
        compiler_params=_params(("parallel",)),
        name="beta_decay",
    )(u, w_in_t, alog, dtb)


def _delta_kernel(heads,
                  q_ref, k_ref, v_ref, z_ref, qh_ref, kh_ref, vh_ref, beta_ref, cum_ref,
                  cq_ref, ck_ref, cv_ref, na_ref,
                  o_ref, state_ref, xs_ref, cumt_ref):
    hp = pl.program_id(0)
    i = pl.program_id(1)

    @pl.when(i == 0)
    def _():
        state_ref[...] = jnp.zeros_like(state_ref)

    first = i == 0
    row_i = lax.broadcasted_iota(jnp.int32, (CHUNK, CHUNK), 0)
    col_i = lax.broadcasted_iota(jnp.int32, (CHUNK, CHUNK), 1)
    causal = row_i >= col_i
    strict = row_i > col_i
    eye = jnp.where(row_i == col_i, 1.0, 0.0)
    merge_masks = []
    for shift in range(int(math.log2(CHUNK))):
        rb = row_i >> shift
        merge_masks.append(jnp.logical_and((rb & 1) == 1, (col_i >> shift) == rb - 1))

    def conv_silu(x_ref, halo_ref, cw_ref):
        xs_ref[0:8, :] = jnp.where(first, 0.0, halo_ref[...])
        xs_ref[8:8 + CHUNK, :] = x_ref[...]
        acc = cw_ref[CONV_K - 1:CONV_K, :] * xs_ref[8:8 + CHUNK, :]
        for back in range(1, CONV_K):
            tap = CONV_K - 1 - back
            acc = acc + cw_ref[tap:tap + 1, :] * xs_ref[8 - back:8 - back + CHUNK, :]
        return _silu(acc)

    q_all = conv_silu(q_ref, qh_ref, cq_ref)
    k_all = conv_silu(k_ref, kh_ref, ck_ref)
    v_all = conv_silu(v_ref, vh_ref, cv_ref)
    beta_t = beta_ref[...]
    cum_tile = cum_ref[...]
    cumt_ref[...] = cum_tile.T

    hs = range(heads)
    sls = [slice(j * LANES, (j + 1) * LANES) for j in hs]
    q, k, v, beta_b, cum, decay = [], [], [], [], [], []
    for j in hs:
        head = hp * heads + j
        q.append(_l2norm(q_all[:, sls[j]]) * (DKA ** -0.5))
        k.append(_l2norm(k_all[:, sls[j]]))
        v.append(v_all[:, sls[j]])
        beta_b.append(jnp.broadcast_to(
            jnp.sum(jnp.where(col_i == head, beta_t, 0.0), axis=1, keepdims=True), (CHUNK, CHUNK)))
        cum.append(jnp.broadcast_to(
            jnp.sum(jnp.where(col_i == head + HA, cum_tile, 0.0), axis=1, keepdims=True),
            (CHUNK, CHUNK)))
        cum_t = jnp.broadcast_to(cumt_ref[pl.ds(head + HA, 1), :], (CHUNK, CHUNK))
        decay.append(jnp.exp(jnp.where(causal, cum[j] - cum_t, NEG_BIG)))
    k16 = [k[j].astype(bf16) for j in hs]
    kb = [k[j] * beta_b[j] for j in hs]
    lmat = [jnp.where(strict, _dot_nt(kb[j].astype(bf16), k16[j]) * decay[j], 0.0) for j in hs]
    intra = [jnp.where(causal, _dot_nt(q[j].astype(bf16), k16[j]) * decay[j], 0.0).astype(bf16)
             for j in hs]
    tn = [jnp.where(merge_masks[0], -lmat[j], 0.0) for j in hs]
    for mask in merge_masks[1:]:
        t_inv = [(eye + tn[j]).astype(bf16) for j in hs]
        half = [_dot(t_inv[j], jnp.where(mask, lmat[j], 0.0).astype(bf16)).astype(bf16) for j in hs]
        tn = [tn[j] - _dot(half[j], t_inv[j]) for j in hs]
    e_g = [jnp.exp(cum[j]) for j in hs]
    g_last = [cum[j][CHUNK - 1:CHUNK, :] for j in hs]
    rhs = [jnp.concatenate([v[j] * beta_b[j], kb[j] * e_g[j]], axis=1) for j in hs]
    uw = [rhs[j] + _dot(tn[j].astype(bf16), rhs[j].astype(bf16)) for j in hs]
    k_dec_t = [(k[j] * jnp.exp(g_last[j] - cum[j])).T.astype(bf16) for j in hs]
    state = [state_ref[j] for j in hs]
    ws = [_dot(jnp.concatenate([uw[j][:, DVA:], q[j] * e_g[j]], axis=0).astype(bf16),
               state[j].astype(bf16)) for j in hs]
    v16 = [(uw[j][:, :DVA] - ws[j][:CHUNK]).astype(bf16) for j in hs]
    o_c = [ws[j][CHUNK:] + _dot(intra[j], v16[j]) for j in hs]
    for j in hs:
        state_ref[j] = state[j] * jnp.exp(g_last[j]) + _dot(k_dec_t[j], v16[j])
    for j in hs:
        o_n = o_c[j] * lax.rsqrt(jnp.mean(o_c[j] * o_c[j], axis=-1, keepdims=True) + NORM_EPS)
        o_ref[:, sls[j]] = (o_n * na_ref[...] * _silu(z_ref[:, sls[j]])).astype(o_ref.dtype)


def _delta(proj_a, bg, conv_w, norm_a, layer):
    s = proj_a.shape[0]
    heads = 16
    hw = heads * LANES
    halo_blocks = CHUNK // 8

    def tile(col0):
        return pl.BlockSpec((CHUNK, hw), lambda hp, i: (i, hp + col0 // hw))

    def halo(col0):
        return pl.BlockSpec(
            (8, hw), lambda hp, i: (jnp.maximum(i * halo_blocks - 1, 0), hp + col0 // hw))

    def cw(col0):
        return pl.BlockSpec((None, CONV_K, hw), lambda hp, i: (layer, 0, hp + col0 // hw))

    return pl.pallas_call(
        functools.partial(_delta_kernel, heads),
        grid=(WA // hw, s // CHUNK),
        in_specs=[tile(0), tile(WA), tile(2 * WA), tile(2 * WA + WVA),
                  halo(0), halo(WA), halo(2 * WA),
                  pl.BlockSpec((CHUNK, LANES), lambda hp, i: (i, 0)),
                  pl.BlockSpec((CHUNK, LANES), lambda hp, i: (i, 1)),
                  cw(0), cw(WA), cw(2 * WA),
                  pl.BlockSpec((None, 1, LANES), lambda hp, i: (layer, 0, 0))],
        out_specs=pl.BlockSpec((CHUNK, hw), lambda hp, i: (i, hp)),
        out_shape=jax.ShapeDtypeStruct((s, WVA), bf16),
        scratch_shapes=[pltpu.VMEM((heads, DKA, DVA), f32),
                        pltpu.VMEM((CHUNK + 8, hw), f32),
                        pltpu.VMEM((LANES, CHUNK), f32)],
        compiler_params=_params(("parallel", "arbitrary")),
        name="gated_delta",
    )(proj_a, proj_a, proj_a, proj_a, proj_a, proj_a, proj_a, bg, bg,
      conv_w, conv_w, conv_w, norm_a)


def _attn_kernel(layer, q_ref, kp_ref, kc_ref, vp_ref, vc_ref, bm_ref, sink_ref, o_ref):
    n = pl.program_id(0)
    lane = lax.broadcasted_iota(jnp.int32, (2 * BLK, LANES), 1)
    low = lane < DHB
    key_j = lax.broadcasted_iota(jnp.int32, (BLK, 2 * BLK), 1)
    no_prev = jnp.logical_and(n == 0, key_j < BLK)
    scale = DHB ** -0.5
    groups_per_tile = LANES // DHB
    pairs_per_group = HB // HB_KV // 2
    k_cat, v_cat = [], []
    for t in range(WKVB // LANES):
        sl = slice(t * LANES, (t + 1) * LANES)
        k_t = jnp.concatenate([kp_ref[:, sl], kc_ref[:, sl]], axis=0)
        v_t = jnp.concatenate([vp_ref[:, sl], vc_ref[:, sl]], axis=0)
        for gg in range(groups_per_tile):
            mine = low if gg == 0 else jnp.logical_not(low)
            k_m = jnp.where(mine, k_t, 0.0)
            v_m = jnp.where(mine, v_t, 0.0)
            k_2 = k_m + pltpu.roll(k_m, DHB, axis=1)
            v_2 = v_m + pltpu.roll(v_m, DHB, axis=1)
            k_cat.append(jnp.concatenate([jnp.where(low, k_2, 0.0),
                                          jnp.where(low, 0.0, k_2)], axis=0).astype(bf16))
            v_cat.append(jnp.concatenate([jnp.where(low, v_2, 0.0),
                                          jnp.where(low, 0.0, v_2)], axis=0).astype(bf16))
    pairs = range(HB // 2)
    qls = [slice(p * LANES, (p + 1) * LANES) for p in pairs]
    scores = [_dot_nt((q_ref[:, qls[p]] * scale).astype(bf16), k_cat[p // pairs_per_group])
              for p in pairs]
    probs = []
    for p in pairs:
        halves = []
        for hh in range(2):
            head = 2 * p + hh
            sh = scores[p][:, hh * 2 * BLK:(hh + 1) * 2 * BLK] + bm_ref[head]
            sh = jnp.where(no_prev, NEG_BIG, sh)
            sink = sink_ref[layer, head]
            mx = jnp.maximum(jnp.max(sh, axis=-1, keepdims=True), sink)
            e = jnp.exp(sh - mx)
            den = jnp.sum(e, axis=-1, keepdims=True) + jnp.exp(sink - mx)
            halves.append((e * (1.0 / den)).astype(bf16))
        probs.append(jnp.concatenate(halves, axis=1))
    for p in pairs:
        o_ref[:, qls[p]] = _dot(probs[p], v_cat[p // pairs_per_group]).astype(o_ref.dtype)


def _attn(proj_b, bias_mask, sinks, layer):
    s = proj_b.shape[0]
    kb0 = WQB // WKVB
    return pl.pallas_call(
        functools.partial(_attn_kernel, layer),
        grid=(s // BLK,),
        in_specs=[pl.BlockSpec((BLK, WQB), lambda n: (n, 0)),
                  pl.BlockSpec((BLK, WKVB), lambda n: (jnp.maximum(n - 1, 0), kb0)),
                  pl.BlockSpec((BLK, WKVB), lambda n: (n, kb0)),
                  pl.BlockSpec((BLK, WKVB), lambda n: (jnp.maximum(n - 1, 0), kb0 + 1)),
                  pl.BlockSpec((BLK, WKVB), lambda n: (n, kb0 + 1)),
                  pl.BlockSpec((HB, BLK, 2 * BLK), lambda n: (0, 0, 0)),
                  pl.BlockSpec(memory_space=pltpu.SMEM)],
        out_specs=pl.BlockSpec((BLK, WQB), lambda n: (n, 0)),
        out_shape=jax.ShapeDtypeStruct((s, WQB), bf16),
        compiler_params=_params(("parallel",)),
        name="swa_attention",
    )(proj_b, proj_b, proj_b, proj_b, proj_b, bias_mask, sinks)


def _t5_bucket(d):
    n_exact = N_BUCKETS // 2
    df = jnp.maximum(d, 1).astype(f32)
    large = n_exact + (jnp.log(df / n_exact) / math.log(MAX_DIST / n_exact)
                       * (N_BUCKETS - n_exact)).astype(jnp.int32)
    large = jnp.minimum(large, N_BUCKETS - 1)
    return jnp.where(d < n_exact, d, large)


def _bias_kernel(rbt_ref, bucket_ref, inside_ref, o_ref):
    n = bucket_ref.shape[1]
    rows = lax.broadcasted_iota(jnp.int32, (N_BUCKETS, n), 0)
    onehot = jnp.where(rows == bucket_ref[...], 1.0, 0.0).astype(bf16)
    r = rbt_ref[...]
    r1 = r.astype(bf16)
    rem = r - r1.astype(f32)
    r2 = rem.astype(bf16)
    r3 = (rem - r2.astype(f32)).astype(bf16)
    table = _dot(r1, onehot) + _dot(r2, onehot) + _dot(r3, onehot)
    o_ref[...] = jnp.where(inside_ref[...] != 0, table, NEG_BIG)


def _band_bias(rel_bias):
    i = jnp.arange(BLK)[:, None]
    j = jnp.arange(2 * BLK)[None, :]
    d = i + BLK - j
    inside = ((d >= 0) & (d < WINDOW)).astype(jnp.int32).reshape(1, -1)
    bucket = _t5_bucket(jnp.clip(d, 0, MAX_DIST - 1)).astype(jnp.int32).reshape(1, -1)
    n = bucket.shape[1]
    tn = 4096
    flat = pl.pallas_call(
        _bias_kernel,
        grid=(n // tn,),
        in_specs=[pl.BlockSpec((HB, N_BUCKETS), lambda c: (0, 0)),
                  pl.BlockSpec((1, tn), lambda c: (0, c)),
                  pl.BlockSpec((1, tn), lambda c: (0, c))],
        out_specs=pl.BlockSpec((HB, tn), lambda c: (0, c)),
        out_shape=jax.ShapeDtypeStruct((HB, n), f32),
        compiler_params=_params(("parallel",)),
        name="band_bias",
    )(rel_bias.astype(f32).T, bucket, inside)
    return flat.reshape(HB, BLK, 2 * BLK)


def kernel(x, c, w_ada, b_ada, ada_table, rel_bias, w_in, conv_w, a_log, dt_bias, norm_a, sinks,
           w_up_a, w_up_b, w_o, ln1_g, ln1_b, w_router, b_router, w_gate_up, b_gate_up, w_down,
           b_down, ln2_g, ln2_b):
    bsz, s, d = x.shape
    assert bsz == 1 and d == D_MODEL and s % BLK == 0
    depth = w_in.shape[0]
    xs = x.reshape(s, d)

    mod_base = _ada(jnp.broadcast_to(c, (8, d)), w_ada, b_ada)[0]
    mod = mod_base.reshape(1, N_MOD, d) + ada_table

    b_gu = b_gate_up.reshape(depth, N_EXPERTS, 1, 2 * D_EXPERT)
    b_rt = b_router.reshape(depth, 1, N_EXPERTS)
    pad_h = ((0, 0), (HA, LANES - 2 * HA))
    alog = jnp.pad(a_log, pad_h).reshape(depth, 1, LANES)
    dtb = jnp.pad(dt_bias, pad_h).reshape(depth, 1, LANES)
    na = norm_a.reshape(depth, 1, DVA)
    g1 = ln1_g.reshape(depth, 1, d)
    b1 = ln1_b.reshape(depth, 1, d)
    g2 = ln2_g.reshape(depth, 1, d)
    b2 = ln2_b.reshape(depth, 1, d)
    bias_mask = _band_bias(rel_bias)
    w_in_t = jnp.swapaxes(w_in, 1, 2)

    def mrow(layer, idx):
        return mod[layer, idx].reshape(1, d)

    u = _modulate(xs, mrow(0, 1), mrow(0, 0))
    for layer in range(depth):
        proj_a = _matmul(u, w_in_t, layer, 0, W_PROJ_A, f32, 512, 1024, "in_proj_a", True)
        proj_b = _matmul(u, w_in_t, layer, OFF_B, W_PROJ_B, f32, 512, 1024, "in_proj_b", True)
        bg = _ba(u, w_in_t, alog, dtb, layer)
        o_a = _delta(proj_a, bg, conv_w, na, layer)
        o_b = _attn(proj_b, bias_mask, sinks, layer)
        merged = _merge(o_a, o_b, w_up_a, w_up_b, proj_b, layer)
        y = _matmul(merged, w_o, layer, 0, d, bf16, 1024, 512, "out_proj")
        xs, u_moe, comb, id_tile, w_tile = _ln_route(
            xs, y, mrow(layer, 2), g1, b1, mrow(layer, 4), mrow(layer, 3), w_router, b_rt, layer)
        dest, tile_expert, n_used, pad_row, pad_len, n_tiles = _plan(id_tile, s)
        us = _dispatch(u_moe, dest, pad_row, pad_len, n_used, n_tiles)
        ys = _experts(us, tile_expert, n_used, w_gate_up, b_gu, w_down, layer)
        nxt = min(layer + 1, depth - 1)
        xs, u = _combine_ln(xs, ys, dest, w_tile, comb, b_down, mrow(layer, 5), g2, b2,
                            mrow(nxt, 1), mrow(nxt, 0), layer)
    return xs.reshape(bsz, s, d)
```

```python
import functools
import math

import jax
import jax.numpy as jnp
import numpy as np
from jax import lax
from jax.experimental import pallas as pl
from jax.experimental.pallas import tpu as pltpu

D_MODEL = 4096
DEPTH = 4
HA = 16
DKA = 128
DVA = 128
CONV_K = 4
HB = 32
HB_KV = 8
DHB = 64
WINDOW = 128
BLK = 128
N_BUCKETS = 32
MAX_DIST = WINDOW
N_EXPERTS = 32
TOP_K = 4
D_EXPERT = 256
SWIGLU_LIMIT = 7.0
SWIGLU_ALPHA = 1.702
DN_ALPHA = (2 * DEPTH) ** 0.25
LN_EPS = 1e-5
NORM_EPS = 1e-6
N_MOD = 6
WA = HA * DKA
WVA = HA * DVA
WQB = HB * DHB
WKVB = HB_KV * DHB
OFF_BA = 2 * WA + 2 * WVA
OFF_B = OFF_BA + 2 * HA
W_PROJ_A = OFF_BA
W_PROJ_B = WQB + 2 * WKVB + 2 * D_MODEL

LANES = 128
CHUNK = 128
VMEM_LIMIT = 56 * 1024 * 1024
NEG_BIG = -1e30

bf16 = jnp.bfloat16
f32 = jnp.float32


def _dot(a, b):
    return jnp.dot(a, b, preferred_element_type=f32)


def _dot_nt(a, b):
    return lax.dot_general(a, b, (((1,), (1,)), ((), ())), preferred_element_type=f32)


def _split2(x):
    hi = x.astype(bf16)
    lo = (x - hi.astype(f32)).astype(bf16)
    return hi, lo


def _dot3(a, b):
    ah, al = _split2(a)
    bh, bl = _split2(b)
    return _dot(ah, bh) + _dot(ah, bl) + _dot(al, bh)


def _params(sem):
    return pltpu.CompilerParams(dimension_semantics=sem, vmem_limit_bytes=VMEM_LIMIT)


def _ada_kernel(c_ref, w_ref, b_ref, o_ref):
    c = c_ref[...]
    a = c * jax.nn.sigmoid(c)
    o_ref[...] = _dot3(a, w_ref[...]) + b_ref[...]


def _ada(c8, w_ada, b_ada):
    n = w_ada.shape[1]
    tn = 512
    return pl.pallas_call(
        _ada_kernel,
        grid=(n // tn,),
        in_specs=[pl.BlockSpec((8, D_MODEL), lambda j: (0, 0)),
                  pl.BlockSpec((D_MODEL, tn), lambda j: (0, j)),
                  pl.BlockSpec((1, tn), lambda j: (0, j))],
        out_specs=pl.BlockSpec((8, tn), lambda j: (0, j)),
        out_shape=jax.ShapeDtypeStruct((8, n), f32),
        compiler_params=_params(("parallel",)),
        name="ada_mod",
    )(c8, w_ada, b_ada.reshape(1, n))


def _modulate_kernel(x_ref, sc_ref, sh_ref, u_ref):
    u_ref[...] = (x_ref[...] * (1.0 + sc_ref[...]) + sh_ref[...]).astype(u_ref.dtype)


def _modulate(x, sc, sh):
    s, d = x.shape
    tm = min(256, s)
    row = pl.BlockSpec((1, d), lambda i: (0, 0))
    return pl.pallas_call(
        _modulate_kernel,
        grid=(s // tm,),
        in_specs=[pl.BlockSpec((tm, d), lambda i: (i, 0)), row, row],
        out_specs=pl.BlockSpec((tm, d), lambda i: (i, 0)),
        out_shape=jax.ShapeDtypeStruct((s, d), bf16),
        compiler_params=_params(("parallel",)),
        name="modulate",
    )(x, sc, sh)


W_SLAB = 512


def _mm_kernel(transposed, a_ref, w_ref, o_ref, wb_ref):
    first = pl.program_id(1) == 0

    @pl.when(first)
    def _():
        if transposed:
            o_ref[...] = _dot_nt(a_ref[...], w_ref[0].astype(bf16)).astype(o_ref.dtype)
            for r0 in range(0, wb_ref.shape[0], W_SLAB):
                rs = slice(r0, r0 + W_SLAB)
                wb_ref[rs, :] = w_ref[0, :, rs].T.astype(bf16)
        else:
            w16 = w_ref[...].astype(bf16)
            o_ref[...] = _dot(a_ref[...], w16).astype(o_ref.dtype)
            wb_ref[...] = w16

    @pl.when(jnp.logical_not(first))
    def _():
        o_ref[...] = _dot(a_ref[...], wb_ref[...]).astype(o_ref.dtype)


def _matmul(a, w, layer, col0, ncols, out_dtype, tm, tn, name, transposed=False):
    m, k = a.shape
    tm = min(tm, m)
    assert ncols % tn == 0 and k % W_SLAB == 0
    if transposed:
        assert col0 % 8 == 0
        w_spec = pl.BlockSpec((pl.Element(1), pl.Element(tn), pl.Element(k)),
                              lambda j, i: (layer, pl.multiple_of(col0 + j * tn, 8), 0))
    else:
        assert col0 % tn == 0
        w_spec = pl.BlockSpec((None, k, tn), lambda j, i: (layer, 0, j + col0 // tn))
    return pl.pallas_call(
        functools.partial(_mm_kernel, transposed),
        grid=(ncols // tn, m // tm),
        in_specs=[pl.BlockSpec((tm, k), lambda j, i: (i, 0)), w_spec],
        out_specs=pl.BlockSpec((tm, tn), lambda j, i: (i, j)),
        out_shape=jax.ShapeDtypeStruct((m, ncols), out_dtype),
        scratch_shapes=[pltpu.VMEM((k, tn), bf16)],
        compiler_params=_params(("arbitrary", "arbitrary")),
        name=name,
    )(a, w)


def _merge_kernel(oa_ref, ob_ref, wa_ref, wb_ref, ga_ref, gb_ref, o_ref, wa16_ref, wb16_ref):
    @pl.when(pl.program_id(1) == 0)
    def _():
        wa16_ref[...] = wa_ref[...].astype(bf16)
        wb16_ref[...] = wb_ref[...].astype(bf16)

    ya = _dot(oa_ref[...], wa16_ref[...])
    yb = _dot(ob_ref[...], wb16_ref[...])
    o_ref[...] = (jax.nn.sigmoid(ga_ref[...]) * ya
                  + jax.nn.sigmoid(gb_ref[...]) * yb).astype(o_ref.dtype)


def _merge(oa, ob, w_up_a, w_up_b, proj_b, layer):
    s = oa.shape[0]
    tm = min(1024, s)
    tn = 512
    ga0 = (WQB + 2 * WKVB) // tn
    gb0 = ga0 + D_MODEL // tn
    return pl.pallas_call(
        _merge_kernel,
        grid=(D_MODEL // tn, s // tm),
        in_specs=[pl.BlockSpec((tm, WVA), lambda j, i: (i, 0)),
                  pl.BlockSpec((tm, WQB), lambda j, i: (i, 0)),
                  pl.BlockSpec((None, WVA, tn), lambda j, i: (layer, 0, j)),
                  pl.BlockSpec((None, WQB, tn), lambda j, i: (layer, 0, j)),
                  pl.BlockSpec((tm, tn), lambda j, i: (i, j + ga0)),
                  pl.BlockSpec((tm, tn), lambda j, i: (i, j + gb0))],
        out_specs=pl.BlockSpec((tm, tn), lambda j, i: (i, j)),
        out_shape=jax.ShapeDtypeStruct((s, D_MODEL), bf16),
        scratch_shapes=[pltpu.VMEM((WVA, tn), bf16), pltpu.VMEM((WQB, tn), bf16)],
        compiler_params=_params(("arbitrary", "arbitrary")),
        name="gated_merge",
    )(oa, ob, w_up_a, w_up_b, proj_b, proj_b)


def _layernorm(z, g, b):
    mu = jnp.mean(z, axis=-1, keepdims=True)
    zc = z - mu
    var = jnp.mean(zc * zc, axis=-1, keepdims=True)
    return zc * lax.rsqrt(var + LN_EPS) * g + b


def _route(logits):
    tm, n_e = logits.shape
    lane = lax.broadcasted_iota(jnp.int32, logits.shape, 1)
    slot = lax.broadcasted_iota(jnp.int32, (tm, LANES), 1)
    work = logits
    ids, vals = [], []
    for _ in range(TOP_K):
        m = jnp.max(work, axis=-1, keepdims=True)
        idx = jnp.min(jnp.where(work == m, lane, n_e), axis=-1, keepdims=True)
        work = jnp.where(lane == idx, -jnp.inf, work)
        ids.append(idx)
        vals.append(m)
    es = [jnp.exp(v - vals[0]) for v in vals]
    inv = 1.0 / functools.reduce(lambda p, q: p + q, es)
    comb = jnp.zeros(logits.shape, f32)
    id_tile = jnp.zeros((tm, LANES), jnp.int32)
    w_tile = jnp.zeros((tm, LANES), f32)
    for r in range(TOP_K):
        w = es[r] * inv
        comb = jnp.where(lane == ids[r], w, comb)
        id_tile = jnp.where(slot == r, ids[r], id_tile)
        w_tile = jnp.where(slot == r, w, w_tile)
    return comb, id_tile, w_tile


def _pack_halves(x):
    n = x.shape[1] // 2
    lo = pltpu.bitcast(x[:, :n].astype(bf16).astype(f32), jnp.uint32)
    hi = pltpu.bitcast(x[:, n:].astype(bf16).astype(f32), jnp.uint32)
    return (lo >> 16) | (hi & jnp.uint32(0xFFFF0000))


def _unpack_halves(w):
    lo = pltpu.bitcast(w << 16, f32).astype(bf16)
    hi = pltpu.bitcast(w & jnp.uint32(0xFFFF0000), f32).astype(bf16)
    return lo, hi


def _unpack_halves_f32(w):
    return pltpu.bitcast(w << 16, f32), pltpu.bitcast(w & jnp.uint32(0xFFFF0000), f32)


def _ln_route_kernel(x_ref, y_ref, gt_ref, g_ref, b_ref, sc_ref, sh_ref, wr_ref, br_ref,
                     xo_ref, uo_ref, comb_ref, id_ref, w_ref):
    z = DN_ALPHA * x_ref[...] + gt_ref[...] * y_ref[...].astype(f32)
    xn = _layernorm(z, g_ref[...], b_ref[...])
    xo_ref[...] = xn
    u = xn * (1.0 + sc_ref[...]) + sh_ref[...]
    uo_ref[...] = _pack_halves(u)
    logits = _dot3(u, wr_ref[...]) + br_ref[...]
    comb_ref[...], id_ref[...], w_ref[...] = _route(logits)


def _ln_route(x, y, gt, g, b, sc, sh, w_router, b_router, layer):
    s, d = x.shape
    tm = min(256, s)
    tile = pl.BlockSpec((tm, d), lambda i: (i, 0))
    row = pl.BlockSpec((1, d), lambda i: (0, 0))
    lrow = pl.BlockSpec((None, 1, d), lambda i: (layer, 0, 0))
    small = pl.BlockSpec((tm, LANES), lambda i: (i, 0))
    packed = pl.BlockSpec((tm, d // 2), lambda i: (i, 0))
    return pl.pallas_call(
        _ln_route_kernel,
        grid=(s // tm,),
        in_specs=[tile, tile, row, lrow, lrow, row, row,
                  pl.BlockSpec((None, d, N_EXPERTS), lambda i: (layer, 0, 0)),
                  pl.BlockSpec((None, 1, N_EXPERTS), lambda i: (layer, 0, 0))],
        out_specs=[tile, packed, pl.BlockSpec((tm, N_EXPERTS), lambda i: (i, 0)), small, small],
        out_shape=[jax.ShapeDtypeStruct((s, d), f32), jax.ShapeDtypeStruct((s, d // 2), jnp.uint32),
                   jax.ShapeDtypeStruct((s, N_EXPERTS), f32),
                   jax.ShapeDtypeStruct((s, LANES), jnp.int32),
                   jax.ShapeDtypeStruct((s, LANES), f32)],
        compiler_params=_params(("parallel",)),
        name="residual_ln_route",
    )(x, y, gt, g, b, sc, sh, w_router, b_router)


MOE_TM = 256
DISP_TM = 256
COMB_TM = 128
PAD_BITS = tuple(1 << p for p in reversed(range(int(math.log2(MOE_TM)))))


def _plan(id_tile, s):
    idx = id_tile[:, :TOP_K]
    experts = jnp.arange(N_EXPERTS, dtype=jnp.int32)
    onehot = idx[:, :, None] == experts[None, None, :]
    mask = jnp.any(onehot, axis=1).astype(jnp.int32)
    incl = jnp.cumsum(mask, axis=0)
    counts = incl[-1]
    padded = (counts + MOE_TM - 1) // MOE_TM * MOE_TM
    ends = jnp.cumsum(padded)
    starts = ends - padded
    dest_all = starts[None, :] + incl - mask
    dest = jnp.sum(jnp.where(onehot, dest_all[:, None, :], 0), axis=-1)
    n_tiles = (TOP_K * s) // MOE_TM + N_EXPERTS
    end_tiles = ends // MOE_TM
    n_used = end_tiles[-1]
    tiles = jnp.arange(n_tiles, dtype=jnp.int32)
    tile_expert = jnp.sum((jnp.minimum(tiles, n_used - 1)[:, None] >= end_tiles[None, :])
                          .astype(jnp.int32), axis=1)
    tile_expert = jnp.minimum(tile_expert, N_EXPERTS - 1)
    return (dest.astype(jnp.int32), tile_expert.astype(jnp.int32),
            n_used.reshape(1).astype(jnp.int32), (starts + counts).astype(jnp.int32),
            (padded - counts).astype(jnp.int32), n_tiles)


def _dispatch_kernel(pad_row_ref, pad_len_ref, used_ref, dest_ref, u_ref, us_ref, zero_ref, sem, zsem):
    i = pl.program_id(0)
    tm = u_ref.shape[0]

    for r in range(tm):
        for k in range(TOP_K):
            row = dest_ref[0, r * TOP_K + k]
            pltpu.make_async_copy(u_ref.at[pl.ds(r, 1)], us_ref.at[pl.ds(row, 1)], sem).start()

    @pl.when(i == pl.num_programs(0) - 1)
    def _():
        zero_ref[...] = jnp.zeros_like(zero_ref)

        def fill(wait, e, carry):
            row = pad_row_ref[e]
            n = pad_len_ref[e]
            for bit in reversed(PAD_BITS):
                size = bit if bit >= 8 else 1
                for part in range(bit // size):
                    @pl.when((n & bit) != 0)
                    def _():
                        at = row + part * size
                        if size >= 8:
                            at = pl.multiple_of(at, 8)
                        cp = pltpu.make_async_copy(zero_ref.at[pl.ds(0, size)],
                                                   us_ref.at[pl.ds(at, size)], zsem)
                        if wait:
                            cp.wait()
                        else:
                            cp.start()
                row = row + (n & bit)
            return carry

        def tail(wait, t, carry):
            tile = used_ref[0] + t

            @pl.when(tile < us_ref.shape[0] // MOE_TM)
            def _():
                for part in range(MOE_TM // zero_ref.shape[0]):
                    at = pl.multiple_of(tile * MOE_TM + part * zero_ref.shape[0], 8)
                    cp = pltpu.make_async_copy(
                        zero_ref, us_ref.at[pl.ds(at, zero_ref.shape[0])], zsem)
                    if wait:
                        cp.wait()
                    else:
                        cp.start()
            return carry

        for wait in (False, True):
            lax.fori_loop(0, N_EXPERTS, functools.partial(fill, wait), 0)
            lax.fori_loop(0, N_EXPERTS, functools.partial(tail, wait), 0)

    for _ in range(TOP_K):
        pltpu.make_async_copy(u_ref, us_ref.at[pl.ds(0, tm)], sem).wait()


def _dispatch(u, dest, pad_row, pad_len, n_used, n_tiles):
    s, d = u.shape
    tm = min(DISP_TM, s)
    dest3 = dest.reshape(s // tm, 1, tm * TOP_K)
    return pl.pallas_call(
        _dispatch_kernel,
        grid_spec=pltpu.PrefetchScalarGridSpec(
            num_scalar_prefetch=3,
            grid=(s // tm,),
            in_specs=[pl.BlockSpec((None, 1, tm * TOP_K), lambda i, *_: (i, 0, 0),
                                   memory_space=pltpu.SMEM),
                      pl.BlockSpec((tm, d), lambda i, *_: (i, 0))],
            out_specs=pl.BlockSpec(memory_space=pl.ANY),
            scratch_shapes=[pltpu.VMEM((MOE_TM // 2, d), u.dtype),
                            pltpu.SemaphoreType.DMA(()), pltpu.SemaphoreType.DMA(())]),
        out_shape=jax.ShapeDtypeStruct((n_tiles * MOE_TM, d), u.dtype),
        compiler_params=_params(("arbitrary",)),
        name="moe_dispatch",
    )(pad_row, pad_len, n_used, dest3, u)


def _experts_kernel(te_ref, nu_ref, x_ref, wgu_ref, bgu_ref, wd_ref, y_ref, wgu16_ref, wd16_ref):
    i = pl.program_id(0)
    new_expert = jnp.logical_or(i == 0, te_ref[i] != te_ref[jnp.maximum(i - 1, 0)])

    @pl.when(new_expert)
    def _():
        for r0 in range(0, wgu_ref.shape[0], W_SLAB):
            rs = slice(r0, r0 + W_SLAB)
            wgu16_ref[rs, :] = wgu_ref[rs, :].astype(bf16)
        wd16_ref[...] = wd_ref[...].astype(bf16)

    @pl.when(i < nu_ref[0])
    def _():
        lo, hi = _unpack_halves(x_ref[...])
        half = lo.shape[1]
        gu = _dot(lo, wgu16_ref[:half, :]) + _dot(hi, wgu16_ref[half:, :]) + bgu_ref[...]
        gate = jnp.minimum(gu[:, :D_EXPERT], SWIGLU_LIMIT)
        up = jnp.clip(gu[:, D_EXPERT:], -SWIGLU_LIMIT, SWIGLU_LIMIT)
        h = (up + 1.0) * gate * jax.nn.sigmoid(SWIGLU_ALPHA * gate)
        y_ref[...] = _pack_halves(_dot(h.astype(bf16), wd16_ref[...]))

    @pl.when(i >= nu_ref[0])
    def _():
        y_ref[...] = jnp.zeros_like(y_ref)


def _experts(us, tile_expert, n_used, w_gate_up, b_gate_up, w_down, layer):
    rows = us.shape[0]
    d = w_down.shape[-1]
    return pl.pallas_call(
        _experts_kernel,
        grid_spec=pltpu.PrefetchScalarGridSpec(
            num_scalar_prefetch=2,
            grid=(rows // MOE_TM,),
            in_specs=[pl.BlockSpec((MOE_TM, d // 2),
                                   lambda i, te, nu: (jnp.minimum(i, nu[0] - 1), 0)),
                      pl.BlockSpec((None, None, d, 2 * D_EXPERT),
                                   lambda i, te, nu: (layer, te[i], 0, 0)),
                      pl.BlockSpec((None, None, 1, 2 * D_EXPERT),
                                   lambda i, te, nu: (layer, te[i], 0, 0)),
                      pl.BlockSpec((None, None, D_EXPERT, d),
                                   lambda i, te, nu: (layer, te[i], 0, 0))],
            out_specs=pl.BlockSpec((MOE_TM, d // 2), lambda i, te, nu: (i, 0)),
            scratch_shapes=[pltpu.VMEM((d, 2 * D_EXPERT), bf16),
                            pltpu.VMEM((D_EXPERT, d), bf16)]),
        out_shape=jax.ShapeDtypeStruct((rows, d // 2), jnp.uint32),
        compiler_params=_params(("arbitrary",)),
        name="moe_experts",
    )(tile_expert, n_used, us, w_gate_up, b_gate_up, w_down)


def _combine_ln_kernel(dest_ref, dnext_ref, x_ref, w_ref, comb_ref, bd_ref, gt_ref, g_ref, b_ref,
                       sc_ref, sh_ref, ys_ref, xo_ref, uo_ref, buf_ref, sem):
    i = pl.program_id(0)
    n = pl.num_programs(0)
    tm = x_ref.shape[0]
    slot = i % 2

    def gather(d_ref, to):
        for r in range(tm):
            for k in range(TOP_K):
                row = d_ref[0, r * TOP_K + k]
                pltpu.make_async_copy(ys_ref.at[pl.ds(row, 1)],
                                      buf_ref.at[to, k, pl.ds(r, 1)], sem.at[to]).start()

    def wait(at):
        for k in range(TOP_K):
            pltpu.make_async_copy(ys_ref.at[pl.ds(0, tm)], buf_ref.at[at, k], sem.at[at]).wait()

    @pl.when(i == 0)
    def _():
        gather(dest_ref, 0)

    w_tile = w_ref[...]
    lane = lax.broadcasted_iota(jnp.int32, w_tile.shape, 1)
    w_ks = [jnp.sum(jnp.where(lane == k, w_tile, 0.0), axis=1, keepdims=True)
            for k in range(TOP_K)]
    bias_term = _dot3(comb_ref[...], bd_ref[...])

    wait(slot)
    gather(dnext_ref, 1 - slot)

    y_lo = y_hi = None
    for k in range(TOP_K):
        lo, hi = _unpack_halves_f32(buf_ref[slot, k])
        y_lo = w_ks[k] * lo if k == 0 else y_lo + w_ks[k] * lo
        y_hi = w_ks[k] * hi if k == 0 else y_hi + w_ks[k] * hi
    y = bias_term + jnp.concatenate([y_lo, y_hi], axis=1)
    z = DN_ALPHA * x_ref[...] + gt_ref[...] * y
    xn = _layernorm(z, g_ref[...], b_ref[...])
    xo_ref[...] = xn
    uo_ref[...] = (xn * (1.0 + sc_ref[...]) + sh_ref[...]).astype(uo_ref.dtype)

    @pl.when(i == n - 1)
    def _():
        wait(1 - slot)


def _combine_ln(x, ys, dest, w_tile, comb, b_down, gt, g, b, sc, sh, layer):
    s, d = x.shape
    tm = min(COMB_TM, s)
    steps = s // tm
    dest3 = dest.reshape(steps, 1, tm * TOP_K)
    tile = pl.BlockSpec((tm, d), lambda i: (i, 0))
    row = pl.BlockSpec((1, d), lambda i: (0, 0))
    lrow = pl.BlockSpec((None, 1, d), lambda i: (layer, 0, 0))
    return pl.pallas_call(
        _combine_ln_kernel,
        grid=(steps,),
        in_specs=[pl.BlockSpec((None, 1, tm * TOP_K), lambda i: (i, 0, 0), memory_space=pltpu.SMEM),
                  pl.BlockSpec((None, 1, tm * TOP_K),
                               lambda i: (jnp.minimum(i + 1, steps - 1), 0, 0),
                               memory_space=pltpu.SMEM),
                  tile,
                  pl.BlockSpec((tm, LANES), lambda i: (i, 0)),
                  pl.BlockSpec((tm, N_EXPERTS), lambda i: (i, 0)),
                  pl.BlockSpec((None, N_EXPERTS, d), lambda i: (layer, 0, 0)),
                  row, lrow, lrow, row, row,
                  pl.BlockSpec(memory_space=pl.ANY)],
        out_specs=[tile, tile],
        out_shape=[jax.ShapeDtypeStruct((s, d), f32), jax.ShapeDtypeStruct((s, d), bf16)],
        scratch_shapes=[pltpu.VMEM((2, TOP_K, tm, d // 2), jnp.uint32),
                        pltpu.SemaphoreType.DMA((2,))],
        compiler_params=_params(("arbitrary",)),
        name="moe_combine_ln",
    )(dest3, dest3, x, w_tile, comb, b_down, gt, g, b, sc, sh, ys)


def _silu(x):
    return x * jax.nn.sigmoid(x)


def _l2norm(x):
    return x * lax.rsqrt(jnp.sum(x * x, axis=-1, keepdims=True) + NORM_EPS)


def _ba_kernel(u_ref, w_ref, alog_ref, dt_ref, o_ref):
    tm = u_ref.shape[0]
    ba = _dot_nt(u_ref[...], w_ref[...].astype(bf16))
    o_ref[:, :LANES] = jax.nn.sigmoid(ba)
    xa = ba + dt_ref[...]
    softplus = jnp.maximum(xa, 0.0) + jnp.log(1.0 + jnp.exp(-jnp.abs(xa)))
    g = -jnp.exp(alog_ref[...]) * softplus
    row_i = lax.broadcasted_iota(jnp.int32, (CHUNK, CHUNK), 0)
    col_i = lax.broadcasted_iota(jnp.int32, (CHUNK, CHUNK), 1)
    tril = jnp.where(row_i >= col_i, 1.0, 0.0).astype(bf16)
    for c in range(tm // CHUNK):
        r = slice(c * CHUNK, (c + 1) * CHUNK)
        g1 = g[r].astype(bf16)
        r1 = g[r] - g1.astype(f32)
        g2 = r1.astype(bf16)
        g3 = (r1 - g2.astype(f32)).astype(bf16)
        o_ref[r, LANES:] = _dot(tril, g1) + _dot(tril, g2) + _dot(tril, g3)


def _ba(u, w_in_t, alog, dtb, layer):
    s, d = u.shape
    tm = min(1024, s)
    lrow = pl.BlockSpec((None, 1, LANES), lambda i: (layer, 0, 0))
    return pl.pallas_call(
        _ba_kernel,
        grid=(s // tm,),
        in_specs=[pl.BlockSpec((tm, d), lambda i: (i, 0)),
                  pl.BlockSpec((None, LANES, d), lambda i: (layer, OFF_BA // LANES, 0)),
                  lrow, lrow],
        out_specs=pl.BlockSpec((tm, 2 * LANES), lambda i: (i, 0)),
        out_shape=jax.ShapeDtypeStruct((s, 2 * LANES), f32),
        compiler_params=_params(("parallel",)),
        name="beta_decay",
    )(u, w_in_t, alog, dtb)


def _delta_kernel(heads,
                  q_ref, k_ref, v_ref, z_ref, qh_ref, kh_ref, vh_ref, beta_ref, cum_ref,
                  cq_ref, ck_ref, cv_ref, na_ref,
                  o_ref, state_ref, xs_ref, cumt_ref):
    hp = pl.program_id(0)
    i = pl.program_id(1)

    @pl.when(i == 0)
    def _():
        state_ref[...] = jnp.zeros_like(state_ref)

    first = i == 0
    row_i = lax.broadcasted_iota(jnp.int32, (CHUNK, CHUNK), 0)
    col_i = lax.broadcasted_iota(jnp.int32, (CHUNK, CHUNK), 1)
    causal = row_i >= col_i
    strict = row_i > col_i
    eye = jnp.where(row_i == col_i, 1.0, 0.0)
    merge_masks = []
    for shift in range(int(math.log2(CHUNK))):
        rb = row_i >> shift
        merge_masks.append(jnp.logical_and((rb & 1) == 1, (col_i >> shift) == rb - 1))

    def conv_silu(x_ref, halo_ref, cw_ref):
        xs_ref[0:8, :] = jnp.where(first, 0.0, halo_ref[...])
        xs_ref[8:8 + CHUNK, :] = x_ref[...]
        acc = cw_ref[CONV_K - 1:CONV_K, :] * xs_ref[8:8 + CHUNK, :]
        for back in range(1, CONV_K):
            tap = CONV_K - 1 - back
            acc = acc + cw_ref[tap:tap + 1, :] * xs_ref[8 - back:8 - back + CHUNK, :]
        return _silu(acc)

    q_all = conv_silu(q_ref, qh_ref, cq_ref)
    k_all = conv_silu(k_ref, kh_ref, ck_ref)
    v_all = conv_silu(v_ref, vh_ref, cv_ref)
    beta_t = beta_ref[...]
    cum_tile = cum_ref[...]
    cumt_ref[...] = cum_tile.T

    hs = range(heads)
    sls = [slice(j * LANES, (j + 1) * LANES) for j in hs]
    q, k, v, beta_b, cum, decay = [], [], [], [], [], []
    for j in hs:
        head = hp * heads + j
        q.append(_l2norm(q_all[:, sls[j]]) * (DKA ** -0.5))
        k.append(_l2norm(k_all[:, sls[j]]))
        v.append(v_all[:, sls[j]])
        beta_b.append(jnp.broadcast_to(
            jnp.sum(jnp.where(col_i == head, beta_t, 0.0), axis=1, keepdims=True), (CHUNK, CHUNK)))
        cum.append(jnp.broadcast_to(
            jnp.sum(jnp.where(col_i == head + HA, cum_tile, 0.0), axis=1, keepdims=True),
            (CHUNK, CHUNK)))
        cum_t = jnp.broadcast_to(cumt_ref[pl.ds(head + HA, 1), :], (CHUNK, CHUNK))
        decay.append(jnp.exp(jnp.where(causal, cum[j] - cum_t, NEG_BIG)))
    k16 = [k[j].astype(bf16) for j in hs]
    kb = [k[j] * beta_b[j] for j in hs]
    lmat = [jnp.where(strict, _dot_nt(kb[j].astype(bf16), k16[j]) * decay[j], 0.0) for j in hs]
    intra = [jnp.where(causal, _dot_nt(q[j].astype(bf16), k16[j]) * decay[j], 0.0).astype(bf16)
             for j in hs]
    tn = [jnp.where(merge_masks[0], -lmat[j], 0.0) for j in hs]
    for mask in merge_masks[1:]:
        t_inv = [(eye + tn[j]).astype(bf16) for j in hs]
        half = [_dot(t_inv[j], jnp.where(mask, lmat[j], 0.0).astype(bf16)).astype(bf16) for j in hs]
        tn = [tn[j] - _dot(half[j], t_inv[j]) for j in hs]
    e_g = [jnp.exp(cum[j]) for j in hs]
    g_last = [cum[j][CHUNK - 1:CHUNK, :] for j in hs]
    rhs = [jnp.concatenate([v[j] * beta_b[j], kb[j] * e_g[j]], axis=1) for j in hs]
    uw = [rhs[j] + _dot(tn[j].astype(bf16), rhs[j].astype(bf16)) for j in hs]
    k_dec_t = [(k[j] * jnp.exp(g_last[j] - cum[j])).T.astype(bf16) for j in hs]
    state = [state_ref[j] for j in hs]
    ws = [_dot(jnp.concatenate([uw[j][:, DVA:], q[j] * e_g[j]], axis=0).astype(bf16),
               state[j].astype(bf16)) for j in hs]
    v16 = [(uw[j][:, :DVA] - ws[j][:CHUNK]).astype(bf16) for j in hs]
    o_c = [ws[j][CHUNK:] + _dot(intra[j], v16[j]) for j in hs]
    for j in hs:
        state_ref[j] = state[j] * jnp.exp(g_last[j]) + _dot(k_dec_t[j], v16[j])
    for j in hs:
        o_n = o_c[j] * lax.rsqrt(jnp.mean(o_c[j] * o_c[j], axis=-1, keepdims=True) + NORM_EPS)
        o_ref[:, sls[j]] = (o_n * na_ref[...] * _silu(z_ref[:, sls[j]])).astype(o_ref.dtype)


def _delta(proj_a, bg, conv_w, norm_a, layer):
    s = proj_a.shape[0]
    heads = 16
    hw = heads * LANES
    halo_blocks = CHUNK // 8

    def tile(col0):
        return pl.BlockSpec((CHUNK, hw), lambda hp, i: (i, hp + col0 // hw))

    def halo(col0):
        return pl.BlockSpec(
            (8, hw), lambda hp, i: (jnp.maximum(i * halo_blocks - 1, 0), hp + col0 // hw))

    def cw(col0):
        return pl.BlockSpec((None, CONV_K, hw), lambda hp, i: (layer, 0, hp + col0 // hw))

    return pl.pallas_call(
        functools.partial(_delta_kernel, heads),
        grid=(WA // hw, s // CHUNK),
        in_specs=[tile(0), tile(WA), tile(2 * WA), tile(2 * WA + WVA),
                  halo(0), halo(WA), halo(2 * WA),
                  pl.BlockSpec((CHUNK, LANES), lambda hp, i: (i, 0)),
                  pl.BlockSpec((CHUNK, LANES), lambda hp, i: (i, 1)),
                  cw(0), cw(WA), cw(2 * WA),
                  pl.BlockSpec((None, 1, LANES), lambda hp, i: (layer, 0, 0))],
        out_specs=pl.BlockSpec((CHUNK, hw), lambda hp, i: (i, hp)),
        out_shape=jax.ShapeDtypeStruct((s, WVA), bf16),
        scratch_shapes=[pltpu.VMEM((heads, DKA, DVA), f32),
                        pltpu.VMEM((CHUNK + 8, hw), f32),
                        pltpu.VMEM((LANES, CHUNK), f32)],
        compiler_params=_params(("parallel", "arbitrary")),
        name="gated_delta",
    )(proj_a, proj_a, proj_a, proj_a, proj_a, proj_a, proj_a, bg, bg,
      conv_w, conv_w, conv_w, norm_a)


def _attn_kernel(layer, q_ref, kp_ref, kc_ref, vp_ref, vc_ref, bm_ref, sink_ref, o_ref):
    n = pl.program_id(0)
    lane = lax.broadcasted_iota(jnp.int32, (2 * BLK, LANES), 1)
    low = lane < DHB
    key_j = lax.broadcasted_iota(jnp.int32, (BLK, 2 * BLK), 1)
    no_prev = jnp.logical_and(n == 0, key_j < BLK)
    scale = DHB ** -0.5
    groups_per_tile = LANES // DHB
    pairs_per_group = HB // HB_KV // 2
    k_cat, v_cat = [], []
    for t in range(WKVB // LANES):
        sl = slice(t * LANES, (t + 1) * LANES)
        k_t = jnp.concatenate([kp_ref[:, sl], kc_ref[:, sl]], axis=0)
        v_t = jnp.concatenate([vp_ref[:, sl], vc_ref[:, sl]], axis=0)
        for gg in range(groups_per_tile):
            mine = low if gg == 0 else jnp.logical_not(low)
            k_m = jnp.where(mine, k_t, 0.0)
            v_m = jnp.where(mine, v_t, 0.0)
            k_2 = k_m + pltpu.roll(k_m, DHB, axis=1)
            v_2 = v_m + pltpu.roll(v_m, DHB, axis=1)
            k_cat.append(jnp.concatenate([jnp.where(low, k_2, 0.0),
                                          jnp.where(low, 0.0, k_2)], axis=0).astype(bf16))
            v_cat.append(jnp.concatenate([jnp.where(low, v_2, 0.0),
                                          jnp.where(low, 0.0, v_2)], axis=0).astype(bf16))
    pairs = range(HB // 2)
    qls = [slice(p * LANES, (p + 1) * LANES) for p in pairs]
    scores = [_dot_nt((q_ref[:, qls[p]] * scale).astype(bf16), k_cat[p // pairs_per_group])
              for p in pairs]
    probs = []
    for p in pairs:
        halves = []
        for hh in range(2):
            head = 2 * p + hh
            sh = scores[p][:, hh * 2 * BLK:(hh + 1) * 2 * BLK] + bm_ref[head]
            sh = jnp.where(no_prev, NEG_BIG, sh)
            sink = sink_ref[layer, head]
            mx = jnp.maximum(jnp.max(sh, axis=-1, keepdims=True), sink)
            e = jnp.exp(sh - mx)
            den = jnp.sum(e, axis=-1, keepdims=True) + jnp.exp(sink - mx)
            halves.append((e * (1.0 / den)).astype(bf16))
        probs.append(jnp.concatenate(halves, axis=1))
    for p in pairs:
        o_ref[:, qls[p]] = _dot(probs[p], v_cat[p // pairs_per_group]).astype(o_ref.dtype)


def _attn(proj_b, bias_mask, sinks, layer):
    s = proj_b.shape[0]
    kb0 = WQB // WKVB
    return pl.pallas_call(
        functools.partial(_attn_kernel, layer),
        grid=(s // BLK,),
        in_specs=[pl.BlockSpec((BLK, WQB), lambda n: (n, 0)),
                  pl.BlockSpec((BLK, WKVB), lambda n: (jnp.maximum(n - 1, 0), kb0)),
                  pl.BlockSpec((BLK, WKVB), lambda n: (n, kb0)),
                  pl.BlockSpec((BLK, WKVB), lambda n: (jnp.maximum(n - 1, 0), kb0 + 1)),
                  pl.BlockSpec((BLK, WKVB), lambda n: (n, kb0 + 1)),
                  pl.BlockSpec((HB, BLK, 2 * BLK), lambda n: (0, 0, 0)),
                  pl.BlockSpec(memory_space=pltpu.SMEM)],
        out_specs=pl.BlockSpec((BLK, WQB), lambda n: (n, 0)),
        out_shape=jax.ShapeDtypeStruct((s, WQB), bf16),
        compiler_params=_params(("parallel",)),
        name="swa_attention",
    )(proj_b, proj_b, proj_b, proj_b, proj_b, bias_mask, sinks)


def _t5_bucket(d):
    n_exact = N_BUCKETS // 2
    df = jnp.maximum(d, 1).astype(f32)
    large = n_exact + (jnp.log(df / n_exact) / math.log(MAX_DIST / n_exact)
                       * (N_BUCKETS - n_exact)).astype(jnp.int32)
    large = jnp.minimum(large, N_BUCKETS - 1)
    return jnp.where(d < n_exact, d, large)


def _bias_kernel(rbt_ref, bucket_ref, inside_ref, o_ref):
    n = bucket_ref.shape[1]
    rows = lax.broadcasted_iota(jnp.int32, (N_BUCKETS, n), 0)
    onehot = jnp.where(rows == bucket_ref[...], 1.0, 0.0).astype(bf16)
    r = rbt_ref[...]
    r1 = r.astype(bf16)
    rem = r - r1.astype(f32)
    r2 = rem.astype(bf16)
    r3 = (rem - r2.astype(f32)).astype(bf16)
    table = _dot(r1, onehot) + _dot(r2, onehot) + _dot(r3, onehot)
    o_ref[...] = jnp.where(inside_ref[...] != 0, table, NEG_BIG)


def _band_bias(rel_bias):
    i = jnp.arange(BLK)[:, None]
    j = jnp.arange(2 * BLK)[None, :]
    d = i + BLK - j
    inside = ((d >= 0) & (d < WINDOW)).astype(jnp.int32).reshape(1, -1)
    bucket = _t5_bucket(jnp.clip(d, 0, MAX_DIST - 1)).astype(jnp.int32).reshape(1, -1)
    n = bucket.shape[1]
    tn = 4096
    flat = pl.pallas_call(
        _bias_kernel,
        grid=(n // tn,),
        in_specs=[pl.BlockSpec((HB, N_BUCKETS), lambda c: (0, 0)),
                  pl.BlockSpec((1, tn), lambda c: (0, c)),
                  pl.BlockSpec((1, tn), lambda c: (0, c))],
        out_specs=pl.BlockSpec((HB, tn), lambda c: (0, c)),
        out_shape=jax.ShapeDtypeStruct((HB, n), f32),
        compiler_params=_params(("parallel",)),
        name="band_bias",
    )(rel_bias.astype(f32).T, bucket, inside)
    return flat.reshape(HB, BLK, 2 * BLK)


def kernel(x, c, w_ada, b_ada, ada_table, rel_bias, w_in, conv_w, a_log, dt_bias, norm_a, sinks,
           w_up_a, w_up_b, w_o, ln1_g, ln1_b, w_router, b_router, w_gate_up, b_gate_up, w_down,
           b_down, ln2_g, ln2_b):
    bsz, s, d = x.shape
    assert bsz == 1 and d == D_MODEL and s % BLK == 0
    depth = w_in.shape[0]
    xs = x.reshape(s, d)

    mod_base = _ada(jnp.broadcast_to(c, (8, d)), w_ada, b_ada)[0]
    mod = mod_base.reshape(1, N_MOD, d) + ada_table

    b_gu = b_gate_up.reshape(depth, N_EXPERTS, 1, 2 * D_EXPERT)
    b_rt = b_router.reshape(depth, 1, N_EXPERTS)
    pad_h = ((0, 0), (HA, LANES - 2 * HA))
    alog = jnp.pad(a_log, pad_h).reshape(depth, 1, LANES)
    dtb = jnp.pad(dt_bias, pad_h).reshape(depth, 1, LANES)
    na = norm_a.reshape(depth, 1, DVA)
    g1 = ln1_g.reshape(depth, 1, d)
    b1 = ln1_b.reshape(depth, 1, d)
    g2 = ln2_g.reshape(depth, 1, d)
    b2 = ln2_b.reshape(depth, 1, d)
    bias_mask = _band_bias(rel_bias)
    w_in_t = jnp.swapaxes(w_in, 1, 2)

    def mrow(layer, idx):
        return mod[layer, idx].reshape(1, d)

    u = _modulate(xs, mrow(0, 1), mrow(0, 0))
    for layer in range(depth):
        proj_a = _matmul(u, w_in_t, layer, 0, W_PROJ_A, f32, 512, 1024, "in_proj_a", True)
        proj_b = _matmul(u, w_in_t, layer, OFF_B, W_PROJ_B, f32, 512, 1024, "in_proj_b", True)
        bg = _ba(u, w_in_t, alog, dtb, layer)
        o_a = _delta(proj_a, bg, conv_w, na, layer)
        o_b = _attn(proj_b, bias_mask, sinks, layer)
        merged = _merge(o_a, o_b, w_up_a, w_up_b, proj_b, layer)
        y = _matmul(merged, w_o, layer, 0, d, bf16, 1024, 512, "out_proj")
        xs, u_moe, comb, id_tile, w_tile = _ln_route(
            xs, y, mrow(layer, 2), g1, b1, mrow(layer, 4), mrow(layer, 3), w_router, b_rt, layer)
        dest, tile_expert, n_used, pad_row, pad_len, n_tiles = _plan(id_tile, s)
        us = _dispatch(u_moe, dest, pad_row, pad_len, n_used, n_tiles)
        ys = _experts(us, tile_expert, n_used, w_gate_up, b_gu, w_down, layer)
        nxt = min(layer + 1, depth - 1)
        xs, u = _combine_ln(xs, ys, dest, w_tile, comb, b_down, mrow(layer, 5), g2, b2,
                            mrow(nxt, 1), mrow(nxt, 0), layer)
    return xs.reshape(bsz, s, d)
```
